```python
import math
import jax, jax.numpy as jnp
from jax import lax
import numpy as np

D_MODEL = 2048
BATCH = 4
SEQ = 4096
DEPTH = 1

MIX_WIDTH = D_MODEL
ATTN_WIDTH = MIX_WIDTH // 2
SSM_WIDTH = MIX_WIDTH - ATTN_WIDTH
HEAD_DIM = 128
N_HEADS = ATTN_WIDTH // HEAD_DIM
DILATED_PATTERNS = ((128, 1), (512, 4), (2048, 16))
N_BUCKETS = 32
MAX_DISTANCE = 2048
SSM_GROUP = 16
N_SSM_GROUPS = SSM_WIDTH // SSM_GROUP
SSM_STATE = 64
D_FF = 5632
EPS = 1e-6
DT_MIN = 0.001
DT_MAX = 0.1
NEG_INF = -1e30

kernel_name = 'hybrid_dilated_attn_s5_macaron'


def rms_norm(x, g):
    xf = x.astype(jnp.float32)
    y = xf * lax.rsqrt(jnp.mean(xf * xf, axis=-1, keepdims=True) + EPS)
    return (y * g.astype(jnp.float32)).astype(x.dtype)


def swiglu_ffn(x, w_gate, w_up, w_down):
    return (jax.nn.silu(x @ w_gate) * (x @ w_up)) @ w_down


def t5_causal_bucket(dist):
    max_exact = N_BUCKETS // 2
    d_f = jnp.maximum(dist, max_exact).astype(jnp.float32)
    large = max_exact + (jnp.log(d_f / max_exact) / math.log(MAX_DISTANCE / max_exact)
                         * (N_BUCKETS - max_exact)).astype(jnp.int32)
    large = jnp.minimum(large, N_BUCKETS - 1)
    return jnp.where(dist < max_exact, dist, large)


def dilated_window_attention(q, k, v, rel_bias, window, dilation):
    bsz, seq, nh, hd = q.shape
    span = window // dilation
    block = span * dilation
    n_blk = -(-seq // block)
    length = n_blk * block
    pad = length - seq

    def to_blocks(t):
        t = jnp.pad(t, ((0, 0), (0, pad), (0, 0), (0, 0)))
        return t.reshape(bsz, n_blk, span, dilation, nh, hd)

    def with_prev(t):
        prev = jnp.pad(t[:, :-1], ((0, 0), (1, 0), (0, 0), (0, 0), (0, 0), (0, 0)))
        return jnp.concatenate([prev, t], axis=2)

    qb = to_blocks(q)
    kc = with_prev(to_blocks(k))
    vc = with_prev(to_blocks(v))
    scores = jnp.einsum('bnidhe,bnjdhe->bndhij', qb, kc,
                        preferred_element_type=jnp.float32) / math.sqrt(hd)
    qi = jnp.arange(span)[:, None]
    kj = jnp.arange(2 * span)[None, :]
    delta = qi + span - kj
    bucket = t5_causal_bucket(jnp.maximum(delta, 0) * dilation)
    bias = jnp.transpose(rel_bias[bucket].astype(jnp.float32), (2, 0, 1))
    blk_idx = jnp.arange(n_blk)[:, None, None]
    valid = (delta >= 0) & (delta <= span) & ((blk_idx > 0) | (kj >= span))
    logits = scores + bias[None, None, None]
    logits = jnp.where(valid[None, :, None, None], logits, NEG_INF)
    m = jnp.max(logits, axis=-1)
    p = jnp.exp(logits - m[..., None])
    s = jnp.sum(p, axis=-1)
    o = jnp.einsum('bndhij,bnjdhe->bnidhe', p.astype(vc.dtype), vc,
                   preferred_element_type=jnp.float32)
    o = o.reshape(bsz, length, nh, hd)[:, :seq]
    m = jnp.transpose(m, (0, 1, 4, 2, 3)).reshape(bsz, length, nh)[:, :seq]
    s = jnp.transpose(s, (0, 1, 4, 2, 3)).reshape(bsz, length, nh)[:, :seq]
    return o, m, s


def mixture_of_dilations(q, k, v, rel_bias):
    results = [dilated_window_attention(q, k, v, rel_bias, w, d) for (w, d) in DILATED_PATTERNS]
    m_all = jnp.stack([r[1] for r in results], axis=0)
    m_max = jnp.max(m_all, axis=0)
    num = 0.0
    den = 0.0
    for (o, m, s) in results:
        w = jnp.exp(m - m_max)
        num = num + w[..., None] * o
        den = den + w * s
    return num / den[..., None]


def complex_affine_combine(e1, e2):
    ar1, ai1, br1, bi1 = e1
    ar2, ai2, br2, bi2 = e2
    ar = ar2 * ar1 - ai2 * ai1
    ai = ar2 * ai1 + ai2 * ar1
    br = ar2 * br1 - ai2 * bi1 + br2
    bi = ar2 * bi1 + ai2 * br1 + bi2
    return (ar, ai, br, bi)


def s5_ssm(u, lambda_re, lambda_im, log_dt, b_re, b_im, c_re, c_im, d_skip):
    bsz, seq, _ = u.shape
    ug = u.astype(jnp.float32).reshape(bsz, seq, N_SSM_GROUPS, SSM_GROUP)
    dt = jnp.exp(log_dt.astype(jnp.float32))[:, None]
    lr = lambda_re.astype(jnp.float32)
    li = lambda_im.astype(jnp.float32)
    mag = jnp.exp(lr * dt)
    a_re = mag * jnp.cos(li * dt)
    a_im = mag * jnp.sin(li * dt)
    zr = a_re - 1.0
    zi = a_im
    lam_sq = lr * lr + li * li
    coef_re = (zr * lr + zi * li) / lam_sq
    coef_im = (zi * lr - zr * li) / lam_sq
    br = b_re.astype(jnp.float32)
    bim = b_im.astype(jnp.float32)
    bbar_re = coef_re[..., None] * br - coef_im[..., None] * bim
    bbar_im = coef_re[..., None] * bim + coef_im[..., None] * br
    bu_re = jnp.einsum('gpc,bsgc->bsgp', bbar_re, ug)
    bu_im = jnp.einsum('gpc,bsgc->bsgp', bbar_im, ug)
    a_re_b = jnp.broadcast_to(a_re, bu_re.shape)
    a_im_b = jnp.broadcast_to(a_im, bu_im.shape)
    _, _, x_re, x_im = lax.associative_scan(complex_affine_combine,
                                            (a_re_b, a_im_b, bu_re, bu_im), axis=1)
    y = (jnp.einsum('gcp,bsgp->bsgc', c_re.astype(jnp.float32), x_re)
         - jnp.einsum('gcp,bsgp->bsgc', c_im.astype(jnp.float32), x_im)
         + d_skip.astype(jnp.float32).reshape(N_SSM_GROUPS, SSM_GROUP) * ug)
    return y.reshape(bsz, seq, SSM_WIDTH)


def setup_inputs(seed: int = 0) -> dict:
    key = jax.random.key(seed)
    ks = jax.random.split(key, 26)
    f32 = jnp.float32

    def nrm(k, shape, scale):
        return jax.random.normal(k, shape, f32) * scale

    def gain(k, shape):
        return 1.0 + 0.01 * jax.random.normal(k, shape, f32)

    L = DEPTH
    x = jax.random.normal(ks[0], (BATCH, SEQ, D_MODEL), f32)
    ffn1_norm = gain(ks[1], (L, D_MODEL))
    ffn1_w_gate = nrm(ks[2], (L, D_MODEL, D_FF), D_MODEL ** -0.5)
    ffn1_w_up = nrm(ks[3], (L, D_MODEL, D_FF), D_MODEL ** -0.5)
    ffn1_w_down = nrm(ks[4], (L, D_FF, D_MODEL), D_FF ** -0.5)
    mix_norm = gain(ks[5], (L, D_MODEL))
    w_in = nrm(ks[6], (L, D_MODEL, 3 * ATTN_WIDTH + SSM_WIDTH), D_MODEL ** -0.5)
    q_norm = gain(ks[7], (L, HEAD_DIM))
    k_norm = gain(ks[8], (L, HEAD_DIM))
    rel_bias = nrm(ks[9], (N_BUCKETS, N_HEADS), 0.1)
    n_idx = jnp.arange(SSM_STATE, dtype=f32)
    ssm_lambda_re = -0.5 + 0.01 * jax.random.normal(ks[10], (L, N_SSM_GROUPS, SSM_STATE), f32)
    ssm_lambda_im = math.pi * n_idx + 0.01 * jax.random.normal(ks[11], (L, N_SSM_GROUPS, SSM_STATE), f32)
    ssm_log_dt = jax.random.uniform(ks[12], (L, N_SSM_GROUPS), f32,
                                    minval=math.log(DT_MIN), maxval=math.log(DT_MAX))
    ssm_b_re = nrm(ks[13], (L, N_SSM_GROUPS, SSM_STATE, SSM_GROUP), (2 * SSM_GROUP) ** -0.5)
    ssm_b_im = nrm(ks[14], (L, N_SSM_GROUPS, SSM_STATE, SSM_GROUP), (2 * SSM_GROUP) ** -0.5)
    ssm_c_re = nrm(ks[15], (L, N_SSM_GROUPS, SSM_GROUP, SSM_STATE), (2 * SSM_STATE) ** -0.5)
    ssm_c_im = nrm(ks[16], (L, N_SSM_GROUPS, SSM_GROUP, SSM_STATE), (2 * SSM_STATE) ** -0.5)
    ssm_d = nrm(ks[17], (L, SSM_WIDTH), 1.0)
    glu_w = nrm(ks[18], (L, SSM_WIDTH, SSM_WIDTH), SSM_WIDTH ** -0.5)
    glu_b = nrm(ks[19], (L, SSM_WIDTH), 0.01)
    w_out = nrm(ks[20], (L, MIX_WIDTH, D_MODEL), MIX_WIDTH ** -0.5)
    ffn2_norm = gain(ks[21], (L, D_MODEL))
    ffn2_w_gate = nrm(ks[22], (L, D_MODEL, D_FF), D_MODEL ** -0.5)
    ffn2_w_up = nrm(ks[23], (L, D_MODEL, D_FF), D_MODEL ** -0.5)
    ffn2_w_down = nrm(ks[24], (L, D_FF, D_MODEL), D_FF ** -0.5)
    return {'x': x, 'ffn1_norm': ffn1_norm, 'ffn1_w_gate': ffn1_w_gate, 'ffn1_w_up': ffn1_w_up,
            'ffn1_w_down': ffn1_w_down, 'mix_norm': mix_norm, 'w_in': w_in, 'q_norm': q_norm,
            'k_norm': k_norm, 'rel_bias': rel_bias, 'ssm_lambda_re': ssm_lambda_re,
            'ssm_lambda_im': ssm_lambda_im, 'ssm_log_dt': ssm_log_dt, 'ssm_b_re': ssm_b_re,
            'ssm_b_im': ssm_b_im, 'ssm_c_re': ssm_c_re, 'ssm_c_im': ssm_c_im, 'ssm_d': ssm_d,
            'glu_w': glu_w, 'glu_b': glu_b, 'w_out': w_out, 'ffn2_norm': ffn2_norm,
            'ffn2_w_gate': ffn2_w_gate, 'ffn2_w_up': ffn2_w_up, 'ffn2_w_down': ffn2_w_down}


def reference(x, ffn1_norm, ffn1_w_gate, ffn1_w_up, ffn1_w_down, mix_norm, w_in, q_norm,
              k_norm, rel_bias, ssm_lambda_re, ssm_lambda_im, ssm_log_dt, ssm_b_re, ssm_b_im,
              ssm_c_re, ssm_c_im, ssm_d, glu_w, glu_b, w_out, ffn2_norm, ffn2_w_gate,
              ffn2_w_up, ffn2_w_down):
    bsz, seq, _ = x.shape
    h = x
    for l in range(DEPTH):
        h = h + 0.5 * swiglu_ffn(rms_norm(h, ffn1_norm[l]), ffn1_w_gate[l], ffn1_w_up[l], ffn1_w_down[l])
        hn = rms_norm(h, mix_norm[l])
        proj = hn @ w_in[l]
        q = proj[..., :ATTN_WIDTH].reshape(bsz, seq, N_HEADS, HEAD_DIM)
        k = proj[..., ATTN_WIDTH:2 * ATTN_WIDTH].reshape(bsz, seq, N_HEADS, HEAD_DIM)
        v = proj[..., 2 * ATTN_WIDTH:3 * ATTN_WIDTH].reshape(bsz, seq, N_HEADS, HEAD_DIM)
        u = proj[..., 3 * ATTN_WIDTH:]
        q = rms_norm(q, q_norm[l])
        k = rms_norm(k, k_norm[l])
        attn = mixture_of_dilations(q, k, v, rel_bias).reshape(bsz, seq, ATTN_WIDTH).astype(h.dtype)
        y = s5_ssm(u, ssm_lambda_re[l], ssm_lambda_im[l], ssm_log_dt[l], ssm_b_re[l], ssm_b_im[l],
                   ssm_c_re[l], ssm_c_im[l], ssm_d[l])
        y = jax.nn.gelu(y)
        y = y * jax.nn.sigmoid(y @ glu_w[l].astype(jnp.float32) + glu_b[l].astype(jnp.float32))
        mixed = jnp.concatenate([attn, y.astype(h.dtype)], axis=-1) @ w_out[l]
        h = h + mixed
        h = h + 0.5 * swiglu_ffn(rms_norm(h, ffn2_norm[l]), ffn2_w_gate[l], ffn2_w_up[l], ffn2_w_down[l])
    return h
```

```python
import functools
import math

import jax
import jax.numpy as jnp
from jax import lax
from jax.experimental import pallas as pl
from jax.experimental.pallas import tpu as pltpu

F32 = jnp.float32
BF16 = jnp.bfloat16

EPS = 1e-6
NEG_INF = -1e30
DILATED_PATTERNS = ((128, 1), (512, 4), (2048, 16))
MAX_DISTANCE = 2048
SSM_CHUNK = 16
LANES = 128
VMEM_LIMIT_BYTES = 60 * 1024 * 1024


def _params(*semantics):
    return pltpu.CompilerParams(dimension_semantics=semantics, vmem_limit_bytes=VMEM_LIMIT_BYTES)


def _tile(total, pref):
    if total <= pref:
        return total
    t = pref - pref % LANES
    while t > 0:
        if total % t == 0:
            return t
        t -= LANES
    return total


def _rms_norm(x, g):
    return x * lax.rsqrt(jnp.mean(x * x, axis=-1, keepdims=True) + EPS) * g


def _ffn_kernel(x_ref, g_ref, wg_ref, wu_ref, wd_ref, o_ref, xn_ref):
    j = pl.program_id(1)

    @pl.when(j == 0)
    def _():
        x = x_ref[...]
        xn_ref[...] = _rms_norm(x, g_ref[...]).astype(BF16)
        o_ref[...] = x

    xn = xn_ref[...]
    a = jnp.dot(xn, wg_ref[...], preferred_element_type=F32)
    b = jnp.dot(xn, wu_ref[...], preferred_element_type=F32)
    h = (0.5 * a * jax.nn.sigmoid(a) * b).astype(BF16)
    o_ref[...] += jnp.dot(h, wd_ref[...], preferred_element_type=F32)


def _ffn(x, gain, wg, wu, wd, *, tm=512, tf=512):
    t, d = x.shape
    f = wg.shape[1]
    tm = _tile(t, tm)
    tf = _tile(f, tf)
    return pl.pallas_call(
        _ffn_kernel,
        grid=(t // tm, f // tf),
        in_specs=[
            pl.BlockSpec((tm, d), lambda i, j: (i, 0)),
            pl.BlockSpec((1, d), lambda i, j: (0, 0)),
            pl.BlockSpec((d, tf), lambda i, j: (0, j)),
            pl.BlockSpec((d, tf), lambda i, j: (0, j)),
            pl.BlockSpec((tf, d), lambda i, j: (j, 0)),
        ],
        out_specs=pl.BlockSpec((tm, d), lambda i, j: (i, 0)),
        out_shape=jax.ShapeDtypeStruct((t, d), F32),
        scratch_shapes=[pltpu.VMEM((tm, d), BF16)],
        compiler_params=_params("parallel", "arbitrary"),
        name="ffn",
    )(x, gain.reshape(1, d), wg, wu, wd)


def _proj_kernel(h_ref, g_ref, w_ref, qkg_ref, o_ref, hn_ref, *, n_norm_tiles, head_dim):
    n = pl.program_id(1)

    @pl.when(n == 0)
    def _():
        hn_ref[...] = _rms_norm(h_ref[...], g_ref[...]).astype(BF16)

    acc = jnp.dot(hn_ref[...], w_ref[...], preferred_element_type=F32)

    @pl.when(n < n_norm_tiles)
    def _():
        for c in range(0, acc.shape[1], head_dim):
            o_ref[:, c:c + head_dim] = _rms_norm(acc[:, c:c + head_dim], qkg_ref[0, :, c:c + head_dim])

    @pl.when(n >= n_norm_tiles)
    def _():
        o_ref[...] = acc


def _proj(h, gain, w_in, q_gain, k_gain, *, attn_width, tm=512):
    t, d = h.shape
    n_out = w_in.shape[1]
    head_dim = q_gain.shape[0]
    tn = attn_width
    assert n_out % tn == 0
    n_tiles = n_out // tn
    reps = tn // head_dim
    qkg = jnp.ones((n_tiles, 1, tn), F32)
    qkg = qkg.at[0, 0].set(jnp.tile(q_gain, reps)).at[1, 0].set(jnp.tile(k_gain, reps))
    tm = _tile(t, tm)
    return pl.pallas_call(
        functools.partial(_proj_kernel, n_norm_tiles=2, head_dim=head_dim),
        grid=(t // tm, n_tiles),
        in_specs=[
            pl.BlockSpec((tm, d), lambda i, n: (i, 0)),
            pl.BlockSpec((1, d), lambda i, n: (0, 0)),
            pl.BlockSpec((d, tn), lambda i, n: (0, n)),
            pl.BlockSpec((1, 1, tn), lambda i, n: (n, 0, 0)),
        ],
        out_specs=pl.BlockSpec((tm, tn), lambda i, n: (i, n)),
        out_shape=jax.ShapeDtypeStruct((t, n_out), F32),
        scratch_shapes=[pltpu.VMEM((tm, d), BF16)],
        compiler_params=_params("parallel", "arbitrary"),
        name="proj",
    )(h, gain.reshape(1, d), w_in, qkg)


def _t5_bucket_tables(n_buckets):
    tables = []
    max_exact = n_buckets // 2
    for window, dilation in DILATED_PATTERNS:
        span = window // dilation
        qi = jnp.arange(span)[:, None]
        kj = jnp.arange(2 * span)[None, :]
        delta = qi + span - kj
        dist = jnp.maximum(delta, 0) * dilation
        d_f = jnp.maximum(dist, max_exact).astype(F32)
        large = max_exact + (jnp.log(d_f / max_exact) / math.log(MAX_DISTANCE / max_exact)
                             * (n_buckets - max_exact)).astype(jnp.int32)
        large = jnp.minimum(large, n_buckets - 1)
        bucket = jnp.where(dist < max_exact, dist, large)
        valid = (delta >= 0) & (delta <= span)
        tables.append(jnp.where(valid, bucket, -1).astype(jnp.int32))
    return jnp.stack(tables)


def _bias_kernel(rb_ref, bucket_ref, o_ref, *, n_buckets):
    h = pl.program_id(0)
    for p in range(bucket_ref.shape[0]):
        bk = bucket_ref[p]
        acc = jnp.full(bk.shape, NEG_INF, F32)
        for b in range(n_buckets):
            acc = jnp.where(bk == b, rb_ref[b, h], acc)
        o_ref[0, p] = acc


def _bias_tables(rel_bias):
    n_buckets, n_heads = rel_bias.shape
    buckets = _t5_bucket_tables(n_buckets)
    n_pat, span, span2 = buckets.shape
    return pl.pallas_call(
        functools.partial(_bias_kernel, n_buckets=n_buckets),
        grid=(n_heads,),
        in_specs=[
            pl.BlockSpec(memory_space=pltpu.SMEM),
            pl.BlockSpec((n_pat, span, span2), lambda h: (0, 0, 0)),
        ],
        out_specs=pl.BlockSpec((1, n_pat, span, span2), lambda h: (h, 0, 0, 0)),
        out_shape=jax.ShapeDtypeStruct((n_heads, n_pat, span, span2), F32),
        compiler_params=_params("arbitrary"),
        name="bias",
    )(rel_bias.astype(F32), buckets)


def _attn_kernel(q_ref, k_ref, v_ref, bias_ref, o_ref, acc_ref, m_ref, l_ref, *, seq, span):
    hd = q_ref.shape[1]
    scale = 1.0 / math.sqrt(hd)
    nt_dims = (((1,), (1,)), ((), ()))

    def rows(start, d):
        return pl.ds(start, span) if d == 1 else pl.ds(start, span, stride=d)

    def block(start, d, pat, has_prev, first):
        cur = rows(start, d)
        q = q_ref[cur, :].astype(BF16)
        kk = k_ref[cur, :].astype(BF16)
        vv = v_ref[cur, :].astype(BF16)
        bias = bias_ref[0, pat]
        if has_prev:
            prev = rows(start - span * d, d)
            kk = jnp.concatenate([k_ref[prev, :].astype(BF16), kk], axis=0)
            vv = jnp.concatenate([v_ref[prev, :].astype(BF16), vv], axis=0)
        else:
            bias = bias[:, span:]
        s = lax.dot_general(q, kk, nt_dims, preferred_element_type=F32) * scale + bias
        m_blk = jnp.max(s, axis=-1, keepdims=True)
        if first:
            m_new = jnp.broadcast_to(m_blk, (span, hd))
        else:
            m_old = m_ref[cur, :]
            m_new = jnp.maximum(m_old, m_blk)
        m_wide = m_new if s.shape[1] == hd else jnp.concatenate([m_new] * (s.shape[1] // hd), axis=1)
        p = jnp.exp(s - m_wide)
        l_blk = jnp.sum(p, axis=-1, keepdims=True)
        pv = jnp.dot(p.astype(BF16), vv, preferred_element_type=F32)
        if first:
            l_ref[cur, :] = jnp.broadcast_to(l_blk, (span, hd))
            acc_ref[cur, :] = pv
        else:
            alpha = jnp.exp(m_old - m_new)
            l_ref[cur, :] = alpha * l_ref[cur, :] + l_blk
            acc_ref[cur, :] = alpha * acc_ref[cur, :] + pv
        m_ref[cur, :] = m_new

    for pat, (window, d) in enumerate(DILATED_PATTERNS):
        assert window // d == span
        blk = span * d
        n_blk = seq // blk
        first = pat == 0

        def head_blocks(r, carry, d=d, pat=pat, first=first):
            block(r, d, pat, False, first)
            return carry

        lax.fori_loop(0, d, head_blocks, 0)

        def tail_blocks(idx, carry, d=d, pat=pat, first=first, blk=blk):
            r = idx % d
            n = idx // d + 1
            block(n * blk + r, d, pat, True, first)
            return carry

        lax.fori_loop(0, d * (n_blk - 1), tail_blocks, 0)

    o_ref[...] = (acc_ref[...] / l_ref[...]).astype(o_ref.dtype)


def _attention(proj, bias, *, batch, seq, n_heads, head_dim):
    span = DILATED_PATTERNS[0][0] // DILATED_PATTERNS[0][1]
    for window, d in DILATED_PATTERNS:
        assert seq % window == 0 and window // d == span
    n_pat = bias.shape[1]
    qkv_spec = lambda off: pl.BlockSpec((seq, head_dim), lambda b, h: (b, off + h))
    return pl.pallas_call(
        functools.partial(_attn_kernel, seq=seq, span=span),
        grid=(batch, n_heads),
        in_specs=[
            qkv_spec(0), qkv_spec(n_heads), qkv_spec(2 * n_heads),
            pl.BlockSpec((1, n_pat, span, 2 * span), lambda b, h: (h, 0, 0, 0)),
        ],
        out_specs=pl.BlockSpec((seq, head_dim), lambda b, h: (b, h)),
        out_shape=jax.ShapeDtypeStruct((batch * seq, n_heads * head_dim), BF16),
        scratch_shapes=[pltpu.VMEM((seq, head_dim), F32)] * 3,
        compiler_params=_params("parallel", "arbitrary"),
        name="attn",
    )(proj, proj, proj, bias)


def _ssm_prep_kernel(lr_row_ref, li_row_ref, ldt_row_ref, bre_ref, bim_ref,
                     lr_col_ref, li_col_ref, ldt_col_ref, cre_ref, cim_ref,
                     m_ref, pe_ref, qc_ref, a_ref, *, chunk, group, state):
    gpt = LANES // group
    ns = gpt * state

    def discretize(lr, li, ldt):
        dt = jnp.exp(ldt)
        mag = jnp.exp(lr * dt)
        return mag * jnp.cos(li * dt), mag * jnp.sin(li * dt)

    lr, li = lr_row_ref[0], li_row_ref[0]
    ar, ai = discretize(lr, li, ldt_row_ref[0])
    zr, zi = ar - 1.0, ai
    lam_sq = lr * lr + li * li
    coef_re = (zr * lr + zi * li) / lam_sq
    coef_im = (zi * lr - zr * li) / lam_sq
    bre, bim = bre_ref[0], bim_ref[0]
    bbar_re = coef_re * bre - coef_im * bim
    bbar_im = coef_re * bim + coef_im * bre

    row_g = lax.broadcasted_iota(jnp.int32, (LANES, ns), 0) // group
    col_g = lax.broadcasted_iota(jnp.int32, (LANES, ns), 1) // state
    mask_in = row_g == col_g

    def expand_in(w):
        tiled = jnp.concatenate([w] * gpt, axis=0)
        return jnp.where(mask_in, tiled, 0.0)

    pr = jnp.ones_like(ar)
    pi = jnp.zeros_like(ai)
    pe0 = None
    for m in range(chunk):
        w_re = pr * bbar_re - pi * bbar_im
        w_im = pr * bbar_im + pi * bbar_re
        blk = jnp.concatenate([expand_in(w_re), expand_in(w_im)], axis=1).astype(BF16)
        if m == 0:
            pe0 = blk
        j = chunk - 1 - m
        pe_ref[0, j * LANES:(j + 1) * LANES, :] = blk
        pr, pi = pr * ar - pi * ai, pr * ai + pi * ar
    a_ref[0] = jnp.concatenate([pr, pi], axis=1)

    acr, aci = discretize(lr_col_ref[0], li_col_ref[0], ldt_col_ref[0])
    cre, cim = cre_ref[0], cim_ref[0]
    row_g2 = lax.broadcasted_iota(jnp.int32, (ns, LANES), 0) // state
    col_g2 = lax.broadcasted_iota(jnp.int32, (ns, LANES), 1) // group
    mask_out = row_g2 == col_g2

    qr = jnp.ones_like(acr)
    qi = jnp.zeros_like(aci)
    for m in range(chunk + 1):
        z_re = jnp.where(mask_out, cre * qr - cim * qi, 0.0)
        z_im = jnp.where(mask_out, -(cre * qi + cim * qr), 0.0)
        qc_m = jnp.concatenate([z_re, z_im], axis=0).astype(BF16)
        if m >= 1:
            qc_ref[0, :, (m - 1) * LANES:m * LANES] = qc_m
        if m < chunk:
            bd = jnp.dot(pe0, qc_m, preferred_element_type=F32).astype(BF16)
            for j in range(chunk - m):
                m_ref[0, j * LANES:(j + 1) * LANES, (j + m) * LANES:(j + m + 1) * LANES] = bd
            if m >= 1:
                zeros = jnp.zeros((LANES, LANES), BF16)
                for j in range(m, chunk):
                    m_ref[0, j * LANES:(j + 1) * LANES, (j - m) * LANES:(j - m + 1) * LANES] = zeros
        qr, qi = qr * acr - qi * aci, qr * aci + qi * acr


def _ssm_prep(lam_re, lam_im, log_dt, b_re, b_im, c_re, c_im):
    n_groups, state, group = b_re.shape
    gpt = LANES // group
    kt = n_groups // gpt
    ns = gpt * state
    chunk = SSM_CHUNK

    def row(a):
        return a.astype(F32).reshape(kt, 1, ns)

    def col(a):
        return jnp.broadcast_to(a.astype(F32).reshape(kt, ns, 1), (kt, ns, LANES))

    ldt = jnp.broadcast_to(log_dt.astype(F32)[:, None], (n_groups, state))
    b_cp = lambda b: jnp.transpose(b.astype(F32), (2, 0, 1)).reshape(group, kt, ns).transpose(1, 0, 2)
    c_pc = lambda c: jnp.tile(jnp.transpose(c.astype(F32), (0, 2, 1)).reshape(kt, ns, group), (1, 1, gpt))

    row_spec = pl.BlockSpec((1, 1, ns), lambda k: (k, 0, 0))
    b_spec = pl.BlockSpec((1, group, ns), lambda k: (k, 0, 0))
    col_spec = pl.BlockSpec((1, ns, LANES), lambda k: (k, 0, 0))
    cl = chunk * LANES
    return pl.pallas_call(
        functools.partial(_ssm_prep_kernel, chunk=chunk, group=group, state=state),
        grid=(kt,),
        in_specs=[row_spec, row_spec, row_spec, b_spec, b_spec,
                  col_spec, col_spec, col_spec, col_spec, col_spec],
        out_specs=[
            pl.BlockSpec((1, cl, cl), lambda k: (k, 0, 0)),
            pl.BlockSpec((1, cl, 2 * ns), lambda k: (k, 0, 0)),
            pl.BlockSpec((1, 2 * ns, cl), lambda k: (k, 0, 0)),
            pl.BlockSpec((1, 1, 2 * ns), lambda k: (k, 0, 0)),
        ],
        out_shape=[
            jax.ShapeDtypeStruct((kt, cl, cl), BF16),
            jax.ShapeDtypeStruct((kt, cl, 2 * ns), BF16),
            jax.ShapeDtypeStruct((kt, 2 * ns, cl), BF16),
            jax.ShapeDtypeStruct((kt, 1, 2 * ns), F32),
        ],
        compiler_params=_params("parallel"),
        name="ssm_prep",
    )(row(lam_re), row(lam_im), row(ldt), b_cp(b_re), b_cp(b_im),
      col(lam_re), col(lam_im), col(ldt), c_pc(c_re), c_pc(c_im))


def _chunk_rows(u_ref, rows, chunk):
    return [u_ref[pl.ds(j, rows, stride=chunk), :] for j in range(chunk)]


def _ssm_state_kernel(u_ref, pe_ref, e_ref, *, chunk):
    rows = e_ref.shape[1]
    u = jnp.concatenate(_chunk_rows(u_ref, rows, chunk), axis=1).astype(BF16)
    e_ref[0] = jnp.dot(u, pe_ref[0], preferred_element_type=F32)


def _ssm_state(proj, pe, *, u_col0, rows):
    t = proj.shape[0]
    kt, cl, ns2 = pe.shape
    chunk = cl // LANES
    n_rows = t // chunk
    return pl.pallas_call(
        functools.partial(_ssm_state_kernel, chunk=chunk),
        grid=(kt, n_rows // rows),
        in_specs=[
            pl.BlockSpec((rows * chunk, LANES), lambda k, i: (i, u_col0 + k)),
            pl.BlockSpec((1, cl, ns2), lambda k, i: (k, 0, 0)),
        ],
        out_specs=pl.BlockSpec((1, rows, ns2), lambda k, i: (k, i, 0)),
        out_shape=jax.ShapeDtypeStruct((kt, n_rows, ns2), F32),
        compiler_params=_params("parallel", "arbitrary"),
        name="ssm_state",
    )(proj, pe)


def _ssm_scan_kernel(e_ref, a_ref, x_ref, *, batch, n_chunks):
    ns = a_ref.shape[2] // 2
    ar = a_ref[0, :, :ns]
    ai = a_ref[0, :, ns:]

    def step(n, carry):
        new = []
        for b in range(batch):
            xr, xi = carry[2 * b], carry[2 * b + 1]
            row = pl.ds(b * n_chunks + n, 1)
            x_ref[0, row, :] = jnp.concatenate([xr, xi], axis=1)
            e = e_ref[0, row, :]
            new.append(ar * xr - ai * xi + e[:, :ns])
            new.append(ar * xi + ai * xr + e[:, ns:])
        return tuple(new)

    zero = jnp.zeros((1, ns), F32)
    lax.fori_loop(0, n_chunks, step, (zero,) * (2 * batch))


def _ssm_scan(e, a_chunk, *, batch):
    kt, n_rows, ns2 = e.shape
    n_chunks = n_rows // batch
    return pl.pallas_call(
        functools.partial(_ssm_scan_kernel, batch=batch, n_chunks=n_chunks),
        grid=(kt,),
        in_specs=[
            pl.BlockSpec((1, n_rows, ns2), lambda k: (k, 0, 0)),
            pl.BlockSpec((1, 1, ns2), lambda k: (k, 0, 0)),
        ],
        out_specs=pl.BlockSpec((1, n_rows, ns2), lambda k: (k, 0, 0)),
        out_shape=jax.ShapeDtypeStruct((kt, n_rows, ns2), F32),
        compiler_params=_params("parallel"),
        name="ssm_scan",
    )(e, a_chunk)


def _ssm_out_kernel(u_ref, x_ref, m_ref, qc_ref, d_ref, o_ref, *, chunk):
    rows = x_ref.shape[1]
    us = _chunk_rows(u_ref, rows, chunk)
    u = jnp.concatenate(us, axis=1).astype(BF16)
    y = jnp.dot(u, m_ref[0], preferred_element_type=F32)
    y = y + jnp.dot(x_ref[0].astype(BF16), qc_ref[0], preferred_element_type=F32)
    d = d_ref[0]
    for j in range(chunk):
        yj = y[:, j * LANES:(j + 1) * LANES] + d * us[j]
        o_ref[pl.ds(j, rows, stride=chunk), :] = jax.nn.gelu(yj)


def _ssm_out(proj, x_prev, m, qc, d_skip, *, u_col0, rows):
    t = proj.shape[0]
    kt, cl, _ = m.shape
    ns2 = qc.shape[1]
    chunk = cl // LANES
    n_rows = t // chunk
    return pl.pallas_call(
        functools.partial(_ssm_out_kernel, chunk=chunk),
        grid=(kt, n_rows // rows),
        in_specs=[
            pl.BlockSpec((rows * chunk, LANES), lambda k, i: (i, u_col0 + k)),
            pl.BlockSpec((1, rows, ns2), lambda k, i: (k, i, 0)),
            pl.BlockSpec((1, cl, cl), lambda k, i: (k, 0, 0)),
            pl.BlockSpec((1, ns2, cl), lambda k, i: (k, 0, 0)),
            pl.BlockSpec((1, 1, LANES), lambda k, i: (k, 0, 0)),
        ],
        out_specs=pl.BlockSpec((rows * chunk, LANES), lambda k, i: (i, k)),
        out_shape=jax.ShapeDtypeStruct((t, kt * LANES), F32),
        compiler_params=_params("parallel", "arbitrary"),
        name="ssm_out",
    )(proj, x_prev, m, qc, d_skip.astype(F32).reshape(kt, 1, LANES))


def _outproj_kernel(h_ref, a_ref, y_ref, gw_ref, gb_ref, wa_ref, wy_ref, o_ref):
    y = y_ref[...]
    z = jnp.dot(y.astype(BF16), gw_ref[...], preferred_element_type=F32) + gb_ref[...]
    yg = (y * jax.nn.sigmoid(z)).astype(BF16)
    mixed = jnp.dot(a_ref[...], wa_ref[...], preferred_element_type=F32)
    mixed = mixed + jnp.dot(yg, wy_ref[...], preferred_element_type=F32)
    o_ref[...] = h_ref[...] + mixed


def _outproj(h, attn, y, glu_w, glu_b, w_out, *, tm=512):
    t, d = h.shape
    wa = attn.shape[1]
    wy = y.shape[1]
    tm = _tile(t, tm)
    const = lambda shape: pl.BlockSpec(shape, lambda i: (0, 0))
    return pl.pallas_call(
        _outproj_kernel,
        grid=(t // tm,),
        in_specs=[
            pl.BlockSpec((tm, d), lambda i: (i, 0)),
            pl.BlockSpec((tm, wa), lambda i: (i, 0)),
            pl.BlockSpec((tm, wy), lambda i: (i, 0)),
            const((wy, wy)), const((1, wy)),
            pl.BlockSpec((wa, d), lambda i: (0, 0)),
            pl.BlockSpec((wy, d), lambda i: (wa // wy, 0)),
        ],
        out_specs=pl.BlockSpec((tm, d), lambda i: (i, 0)),
        out_shape=jax.ShapeDtypeStruct((t, d), F32),
        compiler_params=_params("parallel"),
        name="outproj",
    )(h, attn, y, glu_w, glu_b.astype(F32).reshape(1, wy), w_out, w_out)


def kernel(x, ffn1_norm, ffn1_w_gate, ffn1_w_up, ffn1_w_down, mix_norm, w_in, q_norm, k_norm, rel_bias,
           ssm_lambda_re, ssm_lambda_im, ssm_log_dt, ssm_b_re, ssm_b_im, ssm_c_re, ssm_c_im, ssm_d,
           glu_w, glu_b, w_out, ffn2_norm, ffn2_w_gate, ffn2_w_up, ffn2_w_down):
    batch, seq, d_model = x.shape
    depth = w_in.shape[0]
    head_dim = q_norm.shape[-1]
    n_heads = rel_bias.shape[1]
    attn_width = n_heads * head_dim
    ssm_width = glu_w.shape[-1]
    assert w_in.shape[-1] == 3 * attn_width + ssm_width and attn_width % ssm_width == 0
    assert ssm_width % LANES == 0 and seq % SSM_CHUNK == 0
    u_col0 = 3 * attn_width // LANES
    n_chunks = seq // SSM_CHUNK
    ssm_rows = _tile(batch * n_chunks, 256)

    bias = _bias_tables(rel_bias)
    h = x.reshape(batch * seq, d_model).astype(F32)
    for l in range(depth):
        h = _ffn(h, ffn1_norm[l], ffn1_w_gate[l].astype(BF16), ffn1_w_up[l].astype(BF16),
                 ffn1_w_down[l].astype(BF16))
        proj = _proj(h, mix_norm[l], w_in[l].astype(BF16), q_norm[l], k_norm[l], attn_width=attn_width)
        attn = _attention(proj, bias, batch=batch, seq=seq, n_heads=n_heads, head_dim=head_dim)
        m, pe, qc, a_chunk = _ssm_prep(ssm_lambda_re[l], ssm_lambda_im[l], ssm_log_dt[l],
                                       ssm_b_re[l], ssm_b_im[l], ssm_c_re[l], ssm_c_im[l])
        e = _ssm_state(proj, pe, u_col0=u_col0, rows=ssm_rows)
        x_prev = _ssm_scan(e, a_chunk, batch=batch)
        y = _ssm_out(proj, x_prev, m, qc, ssm_d[l], u_col0=u_col0, rows=ssm_rows)
        h = _outproj(h, attn, y, glu_w[l].astype(BF16), glu_b[l], w_out[l].astype(BF16))
        h = _ffn(h, ffn2_norm[l], ffn2_w_gate[l].astype(BF16), ffn2_w_up[l].astype(BF16),
                 ffn2_w_down[l].astype(BF16))
    return h.reshape(batch, seq, d_model).astype(x.dtype)
```

```python
import functools
import math

import jax
import jax.numpy as jnp
from jax import lax
from jax.experimental import pallas as pl
from jax.experimental.pallas import tpu as pltpu

F32 = jnp.float32
BF16 = jnp.bfloat16

EPS = 1e-6
NEG_INF = -1e30
DILATED_PATTERNS = ((128, 1), (512, 4), (2048, 16))
MAX_DISTANCE = 2048
SSM_CHUNK = 16
LANES = 128
VMEM_LIMIT_BYTES = 60 * 1024 * 1024


def _params(*semantics, flags=None):
    return pltpu.CompilerParams(dimension_semantics=semantics, vmem_limit_bytes=VMEM_LIMIT_BYTES, flags=flags)


def _tile(total, pref):
    if total <= pref:
        return total
    t = pref - pref % LANES
    while t > 0:
        if total % t == 0:
            return t
        t -= LANES
    return total


def _rms_norm(x, g):
    return x * lax.rsqrt(jnp.mean(x * x, axis=-1, keepdims=True) + EPS) * g


def _ffn_kernel(x_ref, g_ref, wg_ref, wu_ref, wd_ref, o_ref, xn_ref):
    j = pl.program_id(1)

    @pl.when(j == 0)
    def _():
        x = x_ref[...]
        xn_ref[...] = _rms_norm(x, g_ref[...]).astype(BF16)
        o_ref[...] = x

    xn = xn_ref[...]
    a = jnp.dot(xn, wg_ref[...], preferred_element_type=F32)
    b = jnp.dot(xn, wu_ref[...], preferred_element_type=F32)
    h = (0.5 * a * jax.nn.sigmoid(a) * b).astype(BF16)
    o_ref[...] += jnp.dot(h, wd_ref[...], preferred_element_type=F32)


def _ffn(x, gain, wg, wu, wd, *, tm=1024, tf=512):
    t, d = x.shape
    f = wg.shape[1]
    tm = _tile(t, tm)
    tf = _tile(f, tf)
    return pl.pallas_call(
        _ffn_kernel,
        grid=(t // tm, f // tf),
        in_specs=[
            pl.BlockSpec((tm, d), lambda i, j: (i, 0)),
            pl.BlockSpec((1, d), lambda i, j: (0, 0)),
            pl.BlockSpec((d, tf), lambda i, j: (0, j)),
            pl.BlockSpec((d, tf), lambda i, j: (0, j)),
            pl.BlockSpec((tf, d), lambda i, j: (j, 0)),
        ],
        out_specs=pl.BlockSpec((tm, d), lambda i, j: (i, 0)),
        out_shape=jax.ShapeDtypeStruct((t, d), F32),
        scratch_shapes=[pltpu.VMEM((tm, d), BF16)],
        compiler_params=_params("parallel", "arbitrary"),
        name="ffn",
    )(x, gain.reshape(1, d), wg, wu, wd)


def _proj_kernel(h_ref, g_ref, w_ref, qkg_ref, o_ref, hn_ref, *, n_norm_tiles, head_dim):
    n = pl.program_id(1)

    @pl.when(n == 0)
    def _():
        hn_ref[...] = _rms_norm(h_ref[...], g_ref[...]).astype(BF16)

    acc = jnp.dot(hn_ref[...], w_ref[...], preferred_element_type=F32)

    @pl.when(n < n_norm_tiles)
    def _():
        for c in range(0, acc.shape[1], head_dim):
            o_ref[:, c:c + head_dim] = _rms_norm(acc[:, c:c + head_dim], qkg_ref[0, :, c:c + head_dim])

    @pl.when(n >= n_norm_tiles)
    def _():
        o_ref[...] = acc


def _proj(h, gain, w_in, q_gain, k_gain, *, attn_width, tm=1024):
    t, d = h.shape
    n_out = w_in.shape[1]
    head_dim = q_gain.shape[0]
    tn = attn_width
    assert n_out % tn == 0
    n_tiles = n_out // tn
    reps = tn // head_dim
    qkg = jnp.ones((n_tiles, 1, tn), F32)
    qkg = qkg.at[0, 0].set(jnp.tile(q_gain, reps)).at[1, 0].set(jnp.tile(k_gain, reps))
    tm = _tile(t, tm)
    return pl.pallas_call(
        functools.partial(_proj_kernel, n_norm_tiles=2, head_dim=head_dim),
        grid=(t // tm, n_tiles),
        in_specs=[
            pl.BlockSpec((tm, d), lambda i, n: (i, 0)),
            pl.BlockSpec((1, d), lambda i, n: (0, 0)),
            pl.BlockSpec((d, tn), lambda i, n: (0, n)),
            pl.BlockSpec((1, 1, tn), lambda i, n: (n, 0, 0)),
        ],
        out_specs=pl.BlockSpec((tm, tn), lambda i, n: (i, n)),
        out_shape=jax.ShapeDtypeStruct((t, n_out), F32),
        scratch_shapes=[pltpu.VMEM((tm, d), BF16)],
        compiler_params=_params("parallel", "arbitrary"),
        name="proj",
    )(h, gain.reshape(1, d), w_in, qkg)


def _t5_bucket_tables(n_buckets):
    tables = []
    max_exact = n_buckets // 2
    for window, dilation in DILATED_PATTERNS:
        span = window // dilation
        qi = jnp.arange(span)[:, None]
        kj = jnp.arange(2 * span)[None, :]
        delta = qi + span - kj
        dist = jnp.maximum(delta, 0) * dilation
        d_f = jnp.maximum(dist, max_exact).astype(F32)
        large = max_exact + (jnp.log(d_f / max_exact) / math.log(MAX_DISTANCE / max_exact)
                             * (n_buckets - max_exact)).astype(jnp.int32)
        large = jnp.minimum(large, n_buckets - 1)
        bucket = jnp.where(dist < max_exact, dist, large)
        valid = (delta >= 0) & (delta <= span)
        tables.append(jnp.where(valid, bucket, -1).astype(jnp.int32))
    return jnp.stack(tables)


def _bias_kernel(rb_ref, bucket_ref, o_ref, *, n_buckets):
    h = pl.program_id(0)
    for p in range(bucket_ref.shape[0]):
        bk = bucket_ref[p]
        acc = jnp.full(bk.shape, NEG_INF, F32)
        for b in range(n_buckets):
            acc = jnp.where(bk == b, rb_ref[b, h] * math.log2(math.e), acc)
        o_ref[0, 2 * p] = acc
        in_prev_half = lax.broadcasted_iota(jnp.int32, bk.shape, 1) < bk.shape[1] // 2
        o_ref[0, 2 * p + 1] = jnp.where(in_prev_half, NEG_INF, acc)


def _bias_tables(rel_bias):
    n_buckets, n_heads = rel_bias.shape
    buckets = _t5_bucket_tables(n_buckets)
    n_pat, span, span2 = buckets.shape
    return pl.pallas_call(
        functools.partial(_bias_kernel, n_buckets=n_buckets),
        grid=(n_heads,),
        in_specs=[
            pl.BlockSpec(memory_space=pltpu.SMEM),
            pl.BlockSpec((n_pat, span, span2), lambda h: (0, 0, 0)),
        ],
        out_specs=pl.BlockSpec((1, 2 * n_pat, span, span2), lambda h: (h, 0, 0, 0)),
        out_shape=jax.ShapeDtypeStruct((n_heads, 2 * n_pat, span, span2), F32),
        compiler_params=_params("arbitrary"),
        name="bias",
    )(rel_bias.astype(F32), buckets)


ATTN_UNROLL = (16, 8, 8)


def _attn_kernel(q_ref, k_ref, v_ref, bias_ref, o_ref,
                 acc_a, m_a, l_a, acc_b, m_b, l_b, q4_ref, k4_ref, v4_ref, *, seq, span):
    hd = q_ref.shape[1]
    scale2 = math.log2(math.e) / math.sqrt(hd)
    nt_dims = (((1,), (1,)), ((), ()))
    g4, g16 = seq // 4, seq // 16
    state_a, state_b = (acc_a, m_a, l_a), (acc_b, m_b, l_b)
    src_nat, src_4 = (q_ref, k_ref, v_ref), (q4_ref, k4_ref, v4_ref)

    def block(src, start, n, stride, pat, state_in, state_out, out_start, save4):
        def ld(ref, first_row):
            if stride == 1:
                return ref[pl.ds(first_row, span), :]
            return ref[pl.ds(first_row, span, stride=stride), :]

        has_prev = jnp.minimum(n, 1)
        prev = start - span * stride * has_prev
        out_rows = pl.ds(out_start, span)
        q32, k32, v32 = (ld(r, start) for r in src)
        if save4:
            q4_ref[out_rows, :] = q32
            k4_ref[out_rows, :] = k32
            v4_ref[out_rows, :] = v32
        q = q32.astype(BF16)
        kk = jnp.concatenate([ld(src[1], prev).astype(BF16), k32.astype(BF16)], axis=0)
        vv = jnp.concatenate([ld(src[2], prev).astype(BF16), v32.astype(BF16)], axis=0)
        bias = bias_ref[0, 2 * pat + 1 - has_prev]
        s = lax.dot_general(q, kk, nt_dims, preferred_element_type=F32) * scale2 + bias
        m_blk = jnp.max(s, axis=-1, keepdims=True)
        if state_in is None:
            m_new = jnp.broadcast_to(m_blk, (span, hd))
        else:
            m_old = ld(state_in[1], start)
            m_new = jnp.maximum(m_old, m_blk)
        p = jnp.exp2(s - jnp.concatenate([m_new, m_new], axis=1)).astype(BF16)
        v_aug = jnp.concatenate([vv, jnp.ones_like(vv)], axis=1)
        pv = jnp.dot(p, v_aug, preferred_element_type=F32)
        o_new, l_new = pv[:, :hd], pv[:, hd:]
        if state_in is not None:
            alpha = jnp.exp2(m_old - m_new)
            o_new = alpha * ld(state_in[0], start) + o_new
            l_new = alpha * ld(state_in[2], start) + l_new
        state_out[0][out_rows, :] = o_new
        state_out[1][out_rows, :] = m_new
        state_out[2][out_rows, :] = l_new

    def run(count, unroll, fn):
        assert count % unroll == 0

        def body(t, carry):
            for u in range(unroll):
                fn(t * unroll + u)
            return carry

        lax.fori_loop(0, count // unroll, body, 0)

    blk4 = 4 * span
    n_blocks = seq // span

    run(n_blocks, ATTN_UNROLL[0],
        lambda i: block(src_nat, i * span, i, 1, 0, None, state_a, i * span, False))

    run(n_blocks, ATTN_UNROLL[1],
        lambda i: block(src_nat, (i // 4) * blk4 + i % 4, i // 4, 4, 1, state_a, state_b,
                        (i % 4) * g4 + (i // 4) * span, True))

    run(n_blocks, ATTN_UNROLL[2],
        lambda i: block(src_4, ((i % 16) // 4) * g4 + (i // 16) * blk4 + i % 4, i // 16, 4, 2,
                        state_b, state_a,
                        ((i % 16) // 4) * g4 + (i % 4) * g16 + (i // 16) * span, False))

    for c in range(16):
        rows = pl.ds((c // 4) * g4 + (c % 4) * g16, g16)
        acc_b[pl.ds((c // 4) * g4 + c % 4, g16, stride=4), :] = acc_a[rows, :] / l_a[rows, :]
    for b in range(4):
        acc_a[pl.ds(b, g4, stride=4), :] = acc_b[pl.ds(b * g4, g4), :]
    o_ref[...] = acc_a[...].astype(o_ref.dtype)


def _attention(proj, bias, *, batch, seq, n_heads, head_dim):
    assert DILATED_PATTERNS == ((128, 1), (512, 4), (2048, 16)), "row orders are built for dilations 1, 4, 16"
    span = 128
    assert seq % DILATED_PATTERNS[-1][0] == 0
    n_tables = bias.shape[1]
    qkv_spec = lambda off: pl.BlockSpec((seq, head_dim), lambda b, h: (b, off + h))
    return pl.pallas_call(
        functools.partial(_attn_kernel, seq=seq, span=span),
        grid=(batch, n_heads),
        in_specs=[
            qkv_spec(0), qkv_spec(n_heads), qkv_spec(2 * n_heads),
            pl.BlockSpec((1, n_tables, span, 2 * span), lambda b, h: (h, 0, 0, 0)),
        ],
        out_specs=pl.BlockSpec((seq, head_dim), lambda b, h: (b, h)),
        out_shape=jax.ShapeDtypeStruct((batch * seq, n_heads * head_dim), BF16),
        scratch_shapes=[pltpu.VMEM((seq, head_dim), F32)] * 9,
        compiler_params=_params("parallel", "arbitrary"),
        name="attn",
    )(proj, proj, proj, bias)


def _ssm_prep_kernel(lr_row_ref, li_row_ref, ldt_row_ref, bre_ref, bim_ref,
                     lr_col_ref, li_col_ref, ldt_col_ref, cre_ref, cim_ref,
                     m_ref, pe_ref, qc_ref, a_ref, *, chunk, group, state):
    gpt = LANES // group
    ns = gpt * state

    def discretize(lr, li, ldt):
        dt = jnp.exp(ldt)
        mag = jnp.exp(lr * dt)
        return mag * jnp.cos(li * dt), mag * jnp.sin(li * dt)

    lr, li = lr_row_ref[0], li_row_ref[0]
    ar, ai = discretize(lr, li, ldt_row_ref[0])
    zr, zi = ar - 1.0, ai
    lam_sq = lr * lr + li * li
    coef_re = (zr * lr + zi * li) / lam_sq
    coef_im = (zi * lr - zr * li) / lam_sq
    bre, bim = bre_ref[0], bim_ref[0]
    bbar_re = coef_re * bre - coef_im * bim
    bbar_im = coef_re * bim + coef_im * bre

    row_g = lax.broadcasted_iota(jnp.int32, (LANES, ns), 0) // group
    col_g = lax.broadcasted_iota(jnp.int32, (LANES, ns), 1) // state
    mask_in = row_g == col_g

    def expand_in(w):
        tiled = jnp.concatenate([w] * gpt, axis=0)
        return jnp.where(mask_in, tiled, 0.0)

    pr = jnp.ones_like(ar)
    pi = jnp.zeros_like(ai)
    pe0 = None
    for m in range(chunk):
        w_re = pr * bbar_re - pi * bbar_im
        w_im = pr * bbar_im + pi * bbar_re
        blk = jnp.concatenate([expand_in(w_re), expand_in(w_im)], axis=1).astype(BF16)
        if m == 0:
            pe0 = blk
        j = chunk - 1 - m
        pe_ref[0, j * LANES:(j + 1) * LANES, :] = blk
        pr, pi = pr * ar - pi * ai, pr * ai + pi * ar
    a_ref[0] = jnp.concatenate([pr, pi], axis=1)

    acr, aci = discretize(lr_col_ref[0], li_col_ref[0], ldt_col_ref[0])
    cre, cim = cre_ref[0], cim_ref[0]
    row_g2 = lax.broadcasted_iota(jnp.int32, (ns, LANES), 0) // state
    col_g2 = lax.broadcasted_iota(jnp.int32, (ns, LANES), 1) // group
    mask_out = row_g2 == col_g2

    qr = jnp.ones_like(acr)
    qi = jnp.zeros_like(aci)
    for m in range(chunk + 1):
        z_re = jnp.where(mask_out, cre * qr - cim * qi, 0.0)
        z_im = jnp.where(mask_out, -(cre * qi + cim * qr), 0.0)
        qc_m = jnp.concatenate([z_re, z_im], axis=0).astype(BF16)
        if m >= 1:
            qc_ref[0, :, (m - 1) * LANES:m * LANES] = qc_m
        if m < chunk:
            bd = jnp.dot(pe0, qc_m, preferred_element_type=F32).astype(BF16)
            for j in range(chunk - m):
                m_ref[0, j * LANES:(j + 1) * LANES, (j + m) * LANES:(j + m + 1) * LANES] = bd
            if m >= 1:
                zeros = jnp.zeros((LANES, LANES), BF16)
                for j in range(m, chunk):
                    m_ref[0, j * LANES:(j + 1) * LANES, (j - m) * LANES:(j - m + 1) * LANES] = zeros
        qr, qi = qr * acr - qi * aci, qr * aci + qi * acr


def _ssm_prep(lam_re, lam_im, log_dt, b_re, b_im, c_re, c_im):
    n_groups, state, group = b_re.shape
    gpt = LANES // group
    kt = n_groups // gpt
    ns = gpt * state
    chunk = SSM_CHUNK

    def row(a):
        return a.astype(F32).reshape(kt, 1, ns)

    def col(a):
        return jnp.broadcast_to(a.astype(F32).reshape(kt, ns, 1), (kt, ns, LANES))

    ldt = jnp.broadcast_to(log_dt.astype(F32)[:, None], (n_groups, state))
    b_cp = lambda b: jnp.transpose(b.astype(F32), (2, 0, 1)).reshape(group, kt, ns).transpose(1, 0, 2)
    c_pc = lambda c: jnp.tile(jnp.transpose(c.astype(F32), (0, 2, 1)).reshape(kt, ns, group), (1, 1, gpt))

    row_spec = pl.BlockSpec((1, 1, ns), lambda k: (k, 0, 0))
    b_spec = pl.BlockSpec((1, group, ns), lambda k: (k, 0, 0))
    col_spec = pl.BlockSpec((1, ns, LANES), lambda k: (k, 0, 0))
    cl = chunk * LANES
    return pl.pallas_call(
        functools.partial(_ssm_prep_kernel, chunk=chunk, group=group, state=state),
        grid=(kt,),
        in_specs=[row_spec, row_spec, row_spec, b_spec, b_spec,
                  col_spec, col_spec, col_spec, col_spec, col_spec],
        out_specs=[
            pl.BlockSpec((1, cl, cl), lambda k: (k, 0, 0)),
            pl.BlockSpec((1, cl, 2 * ns), lambda k: (k, 0, 0)),
            pl.BlockSpec((1, 2 * ns, cl), lambda k: (k, 0, 0)),
            pl.BlockSpec((1, 1, 2 * ns), lambda k: (k, 0, 0)),
        ],
        out_shape=[
            jax.ShapeDtypeStruct((kt, cl, cl), BF16),
            jax.ShapeDtypeStruct((kt, cl, 2 * ns), BF16),
            jax.ShapeDtypeStruct((kt, 2 * ns, cl), BF16),
            jax.ShapeDtypeStruct((kt, 1, 2 * ns), F32),
        ],
        compiler_params=_params("parallel"),
        name="ssm_prep",
    )(row(lam_re), row(lam_im), row(ldt), b_cp(b_re), b_cp(b_im),
      col(lam_re), col(lam_im), col(ldt), c_pc(c_re), c_pc(c_im))


def _chunk_rows(u_ref, rows, chunk):
    return [u_ref[pl.ds(j, rows, stride=chunk), :] for j in range(chunk)]


def _ssm_state_kernel(u_ref, pe_ref, e_ref, *, chunk):
    rows = e_ref.shape[1]
    u = jnp.concatenate(_chunk_rows(u_ref, rows, chunk), axis=1).astype(BF16)
    e_ref[0] = jnp.dot(u, pe_ref[0], preferred_element_type=F32)


def _ssm_state(proj, pe, *, u_col0, rows):
    t = proj.shape[0]
    kt, cl, ns2 = pe.shape
    chunk = cl // LANES
    n_rows = t // chunk
    return pl.pallas_call(
        functools.partial(_ssm_state_kernel, chunk=chunk),
        grid=(kt, n_rows // rows),
        in_specs=[
            pl.BlockSpec((rows * chunk, LANES), lambda k, i: (i, u_col0 + k)),
            pl.BlockSpec((1, cl, ns2), lambda k, i: (k, 0, 0)),
        ],
        out_specs=pl.BlockSpec((1, rows, ns2), lambda k, i: (k, i, 0)),
        out_shape=jax.ShapeDtypeStruct((kt, n_rows, ns2), F32),
        compiler_params=_params("parallel", "arbitrary"),
        name="ssm_state",
    )(proj, pe)


def _ssm_scan_kernel(e_ref, a_ref, x_ref, *, batch, n_chunks):
    ns = a_ref.shape[2] // 2
    ar = a_ref[0, :, :ns]
    ai = a_ref[0, :, ns:]

    def step(n, carry):
        new = []
        for b in range(batch):
            xr, xi = carry[2 * b], carry[2 * b + 1]
            row = pl.ds(b * n_chunks + n, 1)
            x_ref[0, row, :] = jnp.concatenate([xr, xi], axis=1)
            e = e_ref[0, row, :]
            new.append(ar * xr - ai * xi + e[:, :ns])
            new.append(ar * xi + ai * xr + e[:, ns:])
        return tuple(new)

    zero = jnp.zeros((1, ns), F32)
    lax.fori_loop(0, n_chunks, step, (zero,) * (2 * batch))


def _ssm_scan(e, a_chunk, *, batch):
    kt, n_rows, ns2 = e.shape
    n_chunks = n_rows // batch
    return pl.pallas_call(
        functools.partial(_ssm_scan_kernel, batch=batch, n_chunks=n_chunks),
        grid=(kt,),
        in_specs=[
            pl.BlockSpec((1, n_rows, ns2), lambda k: (k, 0, 0)),
            pl.BlockSpec((1, 1, ns2), lambda k: (k, 0, 0)),
        ],
        out_specs=pl.BlockSpec((1, n_rows, ns2), lambda k: (k, 0, 0)),
        out_shape=jax.ShapeDtypeStruct((kt, n_rows, ns2), F32),
        compiler_params=_params("parallel"),
        name="ssm_scan",
    )(e, a_chunk)


def _ssm_out_kernel(u_ref, x_ref, m_ref, qc_ref, d_ref, o_ref, *, chunk):
    rows = x_ref.shape[1]
    us = _chunk_rows(u_ref, rows, chunk)
    u = jnp.concatenate(us, axis=1).astype(BF16)
    y = jnp.dot(u, m_ref[0], preferred_element_type=F32)
    y = y + jnp.dot(x_ref[0].astype(BF16), qc_ref[0], preferred_element_type=F32)
    d = d_ref[0]
    for j in range(chunk):
        yj = y[:, j * LANES:(j + 1) * LANES] + d * us[j]
        o_ref[pl.ds(j, rows, stride=chunk), :] = jax.nn.gelu(yj)


def _ssm_out(proj, x_prev, m, qc, d_skip, *, u_col0, rows):
    t = proj.shape[0]
    kt, cl, _ = m.shape
    ns2 = qc.shape[1]
    chunk = cl // LANES
    n_rows = t // chunk
    return pl.pallas_call(
        functools.partial(_ssm_out_kernel, chunk=chunk),
        grid=(kt, n_rows // rows),
        in_specs=[
            pl.BlockSpec((rows * chunk, LANES), lambda k, i: (i, u_col0 + k)),
            pl.BlockSpec((1, rows, ns2), lambda k, i: (k, i, 0)),
            pl.BlockSpec((1, cl, cl), lambda k, i: (k, 0, 0)),
            pl.BlockSpec((1, ns2, cl), lambda k, i: (k, 0, 0)),
            pl.BlockSpec((1, 1, LANES), lambda k, i: (k, 0, 0)),
        ],
        out_specs=pl.BlockSpec((rows * chunk, LANES), lambda k, i: (i, k)),
        out_shape=jax.ShapeDtypeStruct((t, kt * LANES), F32),
        compiler_params=_params("parallel", "arbitrary"),
        name="ssm_out",
    )(proj, x_prev, m, qc, d_skip.astype(F32).reshape(kt, 1, LANES))


def _outproj_kernel(h_ref, a_ref, y_ref, gw_ref, gb_ref, wa_ref, wy_ref, o_ref):
    y = y_ref[...]
    z = jnp.dot(y.astype(BF16), gw_ref[...], preferred_element_type=F32) + gb_ref[...]
    yg = (y * jax.nn.sigmoid(z)).astype(BF16)
    mixed = jnp.dot(a_ref[...], wa_ref[...], preferred_element_type=F32)
    mixed = mixed + jnp.dot(yg, wy_ref[...], preferred_element_type=F32)
    o_ref[...] = h_ref[...] + mixed


def _outproj(h, attn, y, glu_w, glu_b, w_out, *, tm=512):
    t, d = h.shape
    wa = attn.shape[1]
    wy = y.shape[1]
    tm = _tile(t, tm)
    const = lambda shape: pl.BlockSpec(shape, lambda i: (0, 0))
    return pl.pallas_call(
        _outproj_kernel,
        grid=(t // tm,),
        in_specs=[
            pl.BlockSpec((tm, d), lambda i: (i, 0)),
            pl.BlockSpec((tm, wa), lambda i: (i, 0)),
            pl.BlockSpec((tm, wy), lambda i: (i, 0)),
            const((wy, wy)), const((1, wy)),
            pl.BlockSpec((wa, d), lambda i: (0, 0)),
            pl.BlockSpec((wy, d), lambda i: (wa // wy, 0)),
        ],
        out_specs=pl.BlockSpec((tm, d), lambda i: (i, 0)),
        out_shape=jax.ShapeDtypeStruct((t, d), F32),
        compiler_params=_params("parallel"),
        name="outproj",
    )(h, attn, y, glu_w, glu_b.astype(F32).reshape(1, wy), w_out, w_out)


def kernel(x, ffn1_norm, ffn1_w_gate, ffn1_w_up, ffn1_w_down, mix_norm, w_in, q_norm, k_norm, rel_bias,
           ssm_lambda_re, ssm_lambda_im, ssm_log_dt, ssm_b_re, ssm_b_im, ssm_c_re, ssm_c_im, ssm_d,
           glu_w, glu_b, w_out, ffn2_norm, ffn2_w_gate, ffn2_w_up, ffn2_w_down):
    batch, seq, d_model = x.shape
    depth = w_in.shape[0]
    head_dim = q_norm.shape[-1]
    n_heads = rel_bias.shape[1]
    attn_width = n_heads * head_dim
    ssm_width = glu_w.shape[-1]
    assert w_in.shape[-1] == 3 * attn_width + ssm_width and attn_width % ssm_width == 0
    assert ssm_width % LANES == 0 and seq % SSM_CHUNK == 0
    u_col0 = 3 * attn_width // LANES
    n_chunks = seq // SSM_CHUNK
    ssm_rows = _tile(batch * n_chunks, 256)

    bias = _bias_tables(rel_bias)
    h = x.reshape(batch * seq, d_model).astype(F32)
    for l in range(depth):
        h = _ffn(h, ffn1_norm[l], ffn1_w_gate[l].astype(BF16), ffn1_w_up[l].astype(BF16),
                 ffn1_w_down[l].astype(BF16))
        proj = _proj(h, mix_norm[l], w_in[l].astype(BF16), q_norm[l], k_norm[l], attn_width=attn_width)
        attn = _attention(proj, bias, batch=batch, seq=seq, n_heads=n_heads, head_dim=head_dim)
        m, pe, qc, a_chunk = _ssm_prep(ssm_lambda_re[l], ssm_lambda_im[l], ssm_log_dt[l],
                                       ssm_b_re[l], ssm_b_im[l], ssm_c_re[l], ssm_c_im[l])
        e = _ssm_state(proj, pe, u_col0=u_col0, rows=ssm_rows)
        x_prev = _ssm_scan(e, a_chunk, batch=batch)
        y = _ssm_out(proj, x_prev, m, qc, ssm_d[l], u_col0=u_col0, rows=ssm_rows)
        h = _outproj(h, attn, y, glu_w[l].astype(BF16), glu_b[l], w_out[l].astype(BF16))
        h = _ffn(h, ffn2_norm[l], ffn2_w_gate[l].astype(BF16), ffn2_w_up[l].astype(BF16),
                 ffn2_w_down[l].astype(BF16))
    return h.reshape(batch, seq, d_model).astype(x.dtype)
```

```python
import functools
import math

import jax
import jax.numpy as jnp
from jax import lax
from jax.experimental import pallas as pl
from jax.experimental.pallas import tpu as pltpu

F32 = jnp.float32
BF16 = jnp.bfloat16

EPS = 1e-6
NEG_INF = -1e30
DILATED_PATTERNS = ((128, 1), (512, 4), (2048, 16))
MAX_DISTANCE = 2048
SSM_CHUNK = 16
LANES = 128
VMEM_LIMIT_BYTES = 60 * 1024 * 1024


def _params(*semantics, flags=None):
    return pltpu.CompilerParams(dimension_semantics=semantics, vmem_limit_bytes=VMEM_LIMIT_BYTES, flags=flags)


def _tile(total, pref):
    if total <= pref:
        return total
    t = pref - pref % LANES
    while t > 0:
        if total % t == 0:
            return t
        t -= LANES
    return total


def _rms_norm(x, g):
    return x * lax.rsqrt(jnp.mean(x * x, axis=-1, keepdims=True) + EPS) * g


def _ffn_kernel(x_ref, g_ref, wg_ref, wu_ref, wd_ref, o_ref, xn_ref):
    j = pl.program_id(1)

    @pl.when(j == 0)
    def _():
        x = x_ref[...]
        xn_ref[...] = _rms_norm(x, g_ref[...]).astype(BF16)
        o_ref[...] = x

    xn = xn_ref[...]
    a = jnp.dot(xn, wg_ref[...], preferred_element_type=F32)
    b = jnp.dot(xn, wu_ref[...], preferred_element_type=F32)
    h = (0.5 * a * jax.nn.sigmoid(a) * b).astype(BF16)
    o_ref[...] += jnp.dot(h, wd_ref[...].astype(BF16), preferred_element_type=F32)


def _ffn(x, gain, wg, wu, wd, *, tm=1024, tf=512):
    t, d = x.shape
    f = wg.shape[1]
    tm = _tile(t, tm)
    tf = _tile(f, tf)
    return pl.pallas_call(
        _ffn_kernel,
        grid=(t // tm, f // tf),
        in_specs=[
            pl.BlockSpec((tm, d), lambda i, j: (i, 0)),
            pl.BlockSpec((1, d), lambda i, j: (0, 0)),
            pl.BlockSpec((d, tf), lambda i, j: (0, j)),
            pl.BlockSpec((d, tf), lambda i, j: (0, j)),
            pl.BlockSpec((tf, d), lambda i, j: (j, 0)),
        ],
        out_specs=pl.BlockSpec((tm, d), lambda i, j: (i, 0)),
        out_shape=jax.ShapeDtypeStruct((t, d), F32),
        scratch_shapes=[pltpu.VMEM((tm, d), BF16)],
        compiler_params=_params("parallel", "arbitrary"),
        name="ffn",
    )(x, gain.reshape(1, d), wg, wu, wd)


def _proj_kernel(h_ref, g_ref, w_ref, qkg_ref, o_ref, hn_ref, *, n_norm_tiles, head_dim):
    n = pl.program_id(1)

    @pl.when(n == 0)
    def _():
        hn_ref[...] = _rms_norm(h_ref[...], g_ref[...]).astype(BF16)

    acc = jnp.dot(hn_ref[...], w_ref[...].astype(BF16), preferred_element_type=F32)

    @pl.when(n < n_norm_tiles)
    def _():
        for c in range(0, acc.shape[1], head_dim):
            o_ref[:, c:c + head_dim] = _rms_norm(acc[:, c:c + head_dim], qkg_ref[0, :, c:c + head_dim])

    @pl.when(n >= n_norm_tiles)
    def _():
        o_ref[...] = acc


def _proj(h, gain, w_in, q_gain, k_gain, *, attn_width, tm=1024):
    t, d = h.shape
    n_out = w_in.shape[1]
    head_dim = q_gain.shape[0]
    tn = attn_width
    assert n_out % tn == 0
    n_tiles = n_out // tn
    reps = tn // head_dim
    qkg = jnp.ones((n_tiles, 1, tn), F32)
    qkg = qkg.at[0, 0].set(jnp.tile(q_gain, reps)).at[1, 0].set(jnp.tile(k_gain, reps))
    tm = _tile(t, tm)
    return pl.pallas_call(
        functools.partial(_proj_kernel, n_norm_tiles=2, head_dim=head_dim),
        grid=(t // tm, n_tiles),
        in_specs=[
            pl.BlockSpec((tm, d), lambda i, n: (i, 0)),
            pl.BlockSpec((1, d), lambda i, n: (0, 0)),
            pl.BlockSpec((d, tn), lambda i, n: (0, n)),
            pl.BlockSpec((1, 1, tn), lambda i, n: (n, 0, 0)),
        ],
        out_specs=pl.BlockSpec((tm, tn), lambda i, n: (i, n)),
        out_shape=jax.ShapeDtypeStruct((t, n_out), F32),
        scratch_shapes=[pltpu.VMEM((tm, d), BF16)],
        compiler_params=_params("parallel", "arbitrary"),
        name="proj",
    )(h, gain.reshape(1, d), w_in, qkg)


def _t5_bucket_tables(n_buckets):
    tables = []
    max_exact = n_buckets // 2
    for window, dilation in DILATED_PATTERNS:
        span = window // dilation
        qi = jnp.arange(span)[:, None]
        kj = jnp.arange(2 * span)[None, :]
        delta = qi + span - kj
        dist = jnp.maximum(delta, 0) * dilation
        d_f = jnp.maximum(dist, max_exact).astype(F32)
        large = max_exact + (jnp.log(d_f / max_exact) / math.log(MAX_DISTANCE / max_exact)
                             * (n_buckets - max_exact)).astype(jnp.int32)
        large = jnp.minimum(large, n_buckets - 1)
        bucket = jnp.where(dist < max_exact, dist, large)
        valid = (delta >= 0) & (delta <= span)
        tables.append(jnp.where(valid, bucket, -1).astype(jnp.int32))
    return jnp.stack(tables)


def _bias_kernel(rb_ref, bucket_ref, o_ref, *, n_buckets):
    h = pl.program_id(0)
    for p in range(bucket_ref.shape[0]):
        bk = bucket_ref[p]
        acc = jnp.full(bk.shape, NEG_INF, F32)
        for b in range(n_buckets):
            acc = jnp.where(bk == b, rb_ref[b, h] * math.log2(math.e), acc)
        o_ref[0, 2 * p] = acc
        in_prev_half = lax.broadcasted_iota(jnp.int32, bk.shape, 1) < bk.shape[1] // 2
        o_ref[0, 2 * p + 1] = jnp.where(in_prev_half, NEG_INF, acc)


def _bias_tables(rel_bias):
    n_buckets, n_heads = rel_bias.shape
    buckets = _t5_bucket_tables(n_buckets)
    n_pat, span, span2 = buckets.shape
    return pl.pallas_call(
        functools.partial(_bias_kernel, n_buckets=n_buckets),
        grid=(n_heads,),
        in_specs=[
            pl.BlockSpec(memory_space=pltpu.SMEM),
            pl.BlockSpec((n_pat, span, span2), lambda h: (0, 0, 0)),
        ],
        out_specs=pl.BlockSpec((1, 2 * n_pat, span, span2), lambda h: (h, 0, 0, 0)),
        out_shape=jax.ShapeDtypeStruct((n_heads, 2 * n_pat, span, span2), F32),
        compiler_params=_params("arbitrary"),
        name="bias",
    )(rel_bias.astype(F32), buckets)


ATTN_UNROLL = (16, 8, 8)


def _attn_kernel(q_ref, k_ref, v_ref, bias_ref, o_ref,
                 acc_a, m_a, l_a, acc_b, m_b, l_b, q4_ref, k4_ref, v4_ref, *, seq, span):
    hd = q_ref.shape[1]
    scale2 = math.log2(math.e) / math.sqrt(hd)
    nt_dims = (((1,), (1,)), ((), ()))
    g4, g16 = seq // 4, seq // 16
    state_a, state_b = (acc_a, m_a, l_a), (acc_b, m_b, l_b)
    src_nat, src_4 = (q_ref, k_ref, v_ref), (q4_ref, k4_ref, v4_ref)

    def block(src, start, n, stride, pat, state_in, state_out, out_start, save4):
        def ld(ref, first_row):
            if stride == 1:
                return ref[pl.ds(first_row, span), :]
            return ref[pl.ds(first_row, span, stride=stride), :]

        has_prev = jnp.minimum(n, 1)
        prev = start - span * stride * has_prev
        out_rows = pl.ds(out_start, span)
        q32, k32, v32 = (ld(r, start) for r in src)
        if save4:
            q4_ref[out_rows, :] = q32
            k4_ref[out_rows, :] = k32
            v4_ref[out_rows, :] = v32
        q = q32.astype(BF16)
        kk = jnp.concatenate([ld(src[1], prev).astype(BF16), k32.astype(BF16)], axis=0)
        vv = jnp.concatenate([ld(src[2], prev).astype(BF16), v32.astype(BF16)], axis=0)
        bias = bias_ref[0, 2 * pat + 1 - has_prev]
        s = lax.dot_general(q, kk, nt_dims, preferred_element_type=F32) * scale2 + bias
        m_blk = jnp.max(s, axis=-1, keepdims=True)
        if state_in is None:
            m_new = jnp.broadcast_to(m_blk, (span, hd))
        else:
            m_old = ld(state_in[1], start)
            m_new = jnp.maximum(m_old, m_blk)
        p = jnp.exp2(s - jnp.concatenate([m_new, m_new], axis=1)).astype(BF16)
        v_aug = jnp.concatenate([vv, jnp.ones_like(vv)], axis=1)
        pv = jnp.dot(p, v_aug, preferred_element_type=F32)
        o_new, l_new = pv[:, :hd], pv[:, hd:]
        if state_in is not None:
            alpha = jnp.exp2(m_old - m_new)
            o_new = alpha * ld(state_in[0], start) + o_new
            l_new = alpha * ld(state_in[2], start) + l_new
        state_out[0][out_rows, :] = o_new
        state_out[1][out_rows, :] = m_new
        state_out[2][out_rows, :] = l_new

    def run(count, unroll, fn):
        assert count % unroll == 0

        def body(t, carry):
            for u in range(unroll):
                fn(t * unroll + u)
            return carry

        lax.fori_loop(0, count // unroll, body, 0)

    blk4 = 4 * span
    n_blocks = seq // span

    run(n_blocks, ATTN_UNROLL[0],
        lambda i: block(src_nat, i * span, i, 1, 0, None, state_a, i * span, False))

    run(n_blocks, ATTN_UNROLL[1],
        lambda i: block(src_nat, (i // 4) * blk4 + i % 4, i // 4, 4, 1, state_a, state_b,
                        (i % 4) * g4 + (i // 4) * span, True))

    run(n_blocks, ATTN_UNROLL[2],
        lambda i: block(src_4, ((i % 16) // 4) * g4 + (i // 16) * blk4 + i % 4, i // 16, 4, 2,
                        state_b, state_a,
                        ((i % 16) // 4) * g4 + (i % 4) * g16 + (i // 16) * span, False))

    for c in range(16):
        rows = pl.ds((c // 4) * g4 + (c % 4) * g16, g16)
        acc_b[pl.ds((c // 4) * g4 + c % 4, g16, stride=4), :] = acc_a[rows, :] / l_a[rows, :]
    for b in range(4):
        acc_a[pl.ds(b, g4, stride=4), :] = acc_b[pl.ds(b * g4, g4), :]
    o_ref[...] = acc_a[...].astype(o_ref.dtype)


def _attention(proj, bias, *, batch, seq, n_heads, head_dim):
    assert DILATED_PATTERNS == ((128, 1), (512, 4), (2048, 16)), "row orders are built for dilations 1, 4, 16"
    span = 128
    assert seq % DILATED_PATTERNS[-1][0] == 0
    n_tables = bias.shape[1]
    qkv_spec = lambda off: pl.BlockSpec((seq, head_dim), lambda b, h: (b, off + h))
    return pl.pallas_call(
        functools.partial(_attn_kernel, seq=seq, span=span),
        grid=(batch, n_heads),
        in_specs=[
            qkv_spec(0), qkv_spec(n_heads), qkv_spec(2 * n_heads),
            pl.BlockSpec((1, n_tables, span, 2 * span), lambda b, h: (h, 0, 0, 0)),
        ],
        out_specs=pl.BlockSpec((seq, head_dim), lambda b, h: (b, h)),
        out_shape=jax.ShapeDtypeStruct((batch * seq, n_heads * head_dim), BF16),
        scratch_shapes=[pltpu.VMEM((seq, head_dim), F32)] * 9,
        compiler_params=_params("parallel", "arbitrary"),
        name="attn",
    )(proj, proj, proj, bias)


def _ssm_prep_kernel(lr_row_ref, li_row_ref, ldt_row_ref, bre_ref, bim_ref,
                     lr_col_ref, li_col_ref, ldt_col_ref, cre_ref, cim_ref,
                     m_ref, pe_ref, qc_ref, a_ref, *, chunk, group, state):
    gpt = LANES // group
    ns = gpt * state

    def discretize(lr, li, ldt):
        dt = jnp.exp(ldt)
        mag = jnp.exp(lr * dt)
        return mag * jnp.cos(li * dt), mag * jnp.sin(li * dt)

    lr, li = lr_row_ref[0], li_row_ref[0]
    ar, ai = discretize(lr, li, ldt_row_ref[0])
    zr, zi = ar - 1.0, ai
    lam_sq = lr * lr + li * li
    coef_re = (zr * lr + zi * li) / lam_sq
    coef_im = (zi * lr - zr * li) / lam_sq
    bre, bim = bre_ref[0], bim_ref[0]
    bbar_re = coef_re * bre - coef_im * bim
    bbar_im = coef_re * bim + coef_im * bre

    row_g = lax.broadcasted_iota(jnp.int32, (LANES, ns), 0) // group
    col_g = lax.broadcasted_iota(jnp.int32, (LANES, ns), 1) // state
    mask_in = row_g == col_g

    def expand_in(w):
        tiled = jnp.concatenate([w] * gpt, axis=0)
        return jnp.where(mask_in, tiled, 0.0)

    pr = jnp.ones_like(ar)
    pi = jnp.zeros_like(ai)
    pe0 = None
    for m in range(chunk):
        w_re = pr * bbar_re - pi * bbar_im
        w_im = pr * bbar_im + pi * bbar_re
        blk = jnp.concatenate([expand_in(w_re), expand_in(w_im)], axis=1).astype(BF16)
        if m == 0:
            pe0 = blk
        j = chunk - 1 - m
        pe_ref[0, j * LANES:(j + 1) * LANES, :] = blk
        pr, pi = pr * ar - pi * ai, pr * ai + pi * ar
    a_ref[0] = jnp.concatenate([pr, pi], axis=1)

    acr, aci = discretize(lr_col_ref[0], li_col_ref[0], ldt_col_ref[0])
    cre, cim = cre_ref[0], cim_ref[0]
    row_g2 = lax.broadcasted_iota(jnp.int32, (ns, LANES), 0) // state
    col_g2 = lax.broadcasted_iota(jnp.int32, (ns, LANES), 1) // group
    mask_out = row_g2 == col_g2

    qr = jnp.ones_like(acr)
    qi = jnp.zeros_like(aci)
    for m in range(chunk + 1):
        z_re = jnp.where(mask_out, cre * qr - cim * qi, 0.0)
        z_im = jnp.where(mask_out, -(cre * qi + cim * qr), 0.0)
        qc_m = jnp.concatenate([z_re, z_im], axis=0).astype(BF16)
        if m >= 1:
            qc_ref[0, :, (m - 1) * LANES:m * LANES] = qc_m
        if m < chunk:
            bd = jnp.dot(pe0, qc_m, preferred_element_type=F32).astype(BF16)
            for j in range(chunk - m):
                m_ref[0, j * LANES:(j + 1) * LANES, (j + m) * LANES:(j + m + 1) * LANES] = bd
            if m >= 1:
                zeros = jnp.zeros((LANES, LANES), BF16)
                for j in range(m, chunk):
                    m_ref[0, j * LANES:(j + 1) * LANES, (j - m) * LANES:(j - m + 1) * LANES] = zeros
        qr, qi = qr * acr - qi * aci, qr * aci + qi * acr


def _ssm_prep(lam_re, lam_im, log_dt, b_re, b_im, c_re, c_im):
    n_groups, state, group = b_re.shape
    gpt = LANES // group
    kt = n_groups // gpt
    ns = gpt * state
    chunk = SSM_CHUNK

    def row(a):
        return a.astype(F32).reshape(kt, 1, ns)

    def col(a):
        return jnp.broadcast_to(a.astype(F32).reshape(kt, ns, 1), (kt, ns, LANES))

    ldt = jnp.broadcast_to(log_dt.astype(F32)[:, None], (n_groups, state))
    b_cp = lambda b: jnp.transpose(b.astype(F32), (2, 0, 1)).reshape(group, kt, ns).transpose(1, 0, 2)
    c_pc = lambda c: jnp.tile(jnp.transpose(c.astype(F32), (0, 2, 1)).reshape(kt, ns, group), (1, 1, gpt))

    row_spec = pl.BlockSpec((1, 1, ns), lambda k: (k, 0, 0))
    b_spec = pl.BlockSpec((1, group, ns), lambda k: (k, 0, 0))
    col_spec = pl.BlockSpec((1, ns, LANES), lambda k: (k, 0, 0))
    cl = chunk * LANES
    return pl.pallas_call(
        functools.partial(_ssm_prep_kernel, chunk=chunk, group=group, state=state),
        grid=(kt,),
        in_specs=[row_spec, row_spec, row_spec, b_spec, b_spec,
                  col_spec, col_spec, col_spec, col_spec, col_spec],
        out_specs=[
            pl.BlockSpec((1, cl, cl), lambda k: (k, 0, 0)),
            pl.BlockSpec((1, cl, 2 * ns), lambda k: (k, 0, 0)),
            pl.BlockSpec((1, 2 * ns, cl), lambda k: (k, 0, 0)),
            pl.BlockSpec((1, 1, 2 * ns), lambda k: (k, 0, 0)),
        ],
        out_shape=[
            jax.ShapeDtypeStruct((kt, cl, cl), BF16),
            jax.ShapeDtypeStruct((kt, cl, 2 * ns), BF16),
            jax.ShapeDtypeStruct((kt, 2 * ns, cl), BF16),
            jax.ShapeDtypeStruct((kt, 1, 2 * ns), F32),
        ],
        compiler_params=_params("parallel"),
        name="ssm_prep",
    )(row(lam_re), row(lam_im), row(ldt), b_cp(b_re), b_cp(b_im),
      col(lam_re), col(lam_im), col(ldt), c_pc(c_re), c_pc(c_im))


def _chunk_rows(u_ref, rows, chunk):
    return [u_ref[pl.ds(j, rows, stride=chunk), :] for j in range(chunk)]


def _batch_rows(first, count, batch):
    return pl.ds(first, count) if batch == 1 else pl.ds(first, count, stride=batch)


def _ssm_state_kernel(u_ref, pe_ref, e_ref, *, chunk, batch):
    n_slabs, n_rows = e_ref.shape[1], e_ref.shape[2]
    n_chunks = n_rows // batch
    u = jnp.concatenate(_chunk_rows(u_ref, n_rows, chunk), axis=1).astype(BF16)
    e = jnp.dot(u, pe_ref[0], preferred_element_type=F32)
    for b in range(batch):
        for s in range(n_slabs):
            e_ref[0, s, _batch_rows(b, n_chunks, batch), :] = (
                e[b * n_chunks:(b + 1) * n_chunks, s * LANES:(s + 1) * LANES])


def _ssm_state(proj, pe, *, u_col0, batch):
    t = proj.shape[0]
    kt, cl, ns2 = pe.shape
    chunk = cl // LANES
    n_rows = t // chunk
    n_slabs = ns2 // LANES
    return pl.pallas_call(
        functools.partial(_ssm_state_kernel, chunk=chunk, batch=batch),
        grid=(kt,),
        in_specs=[
            pl.BlockSpec((t, LANES), lambda k: (0, u_col0 + k)),
            pl.BlockSpec((1, cl, ns2), lambda k: (k, 0, 0)),
        ],
        out_specs=pl.BlockSpec((1, n_slabs, n_rows, LANES), lambda k: (k, 0, 0, 0)),
        out_shape=jax.ShapeDtypeStruct((kt, n_slabs, n_rows, LANES), F32),
        compiler_params=_params("parallel"),
        name="ssm_state",
    )(proj, pe)


def _ssm_scan_kernel(e_ref, a_ref, x_ref, *, batch):
    n_slabs, n_rows = e_ref.shape[1], e_ref.shape[2]
    half = n_slabs // 2
    n_chunks = n_rows // batch
    ar = jnp.broadcast_to(a_ref[0, :half], (half, batch, LANES))
    ai = jnp.broadcast_to(a_ref[0, half:], (half, batch, LANES))

    def step(n, carry):
        xr, xi = carry
        rows = pl.ds(n * batch, batch)
        x_ref[0, :half, rows, :] = xr
        x_ref[0, half:, rows, :] = xi
        er = e_ref[0, :half, rows, :]
        ei = e_ref[0, half:, rows, :]
        return ar * xr - ai * xi + er, ar * xi + ai * xr + ei

    zero = jnp.zeros((half, batch, LANES), F32)
    lax.fori_loop(0, n_chunks, step, (zero, zero))


def _ssm_scan(e, a_chunk, *, batch):
    kt, n_slabs, n_rows, _ = e.shape
    spec = pl.BlockSpec((1, n_slabs, n_rows, LANES), lambda k: (k, 0, 0, 0))
    return pl.pallas_call(
        functools.partial(_ssm_scan_kernel, batch=batch),
        grid=(kt,),
        in_specs=[spec, pl.BlockSpec((1, n_slabs, 1, LANES), lambda k: (k, 0, 0, 0))],
        out_specs=spec,
        out_shape=jax.ShapeDtypeStruct(e.shape, F32),
        compiler_params=_params("parallel"),
        name="ssm_scan",
    )(e, a_chunk.reshape(kt, n_slabs, 1, LANES))


def _ssm_out_kernel(u_ref, x_ref, m_ref, qc_ref, d_ref, o_ref, *, chunk, batch, seqs):
    i = pl.program_id(1)
    rows = u_ref.shape[0] // chunk
    n_chunks = rows // seqs
    n_slabs = x_ref.shape[1]
    us = _chunk_rows(u_ref, rows, chunk)
    u = jnp.concatenate(us, axis=1).astype(BF16)
    x = jnp.concatenate(
        [jnp.concatenate([x_ref[0, sl, _batch_rows(i * seqs + s, n_chunks, batch), :] for sl in range(n_slabs)],
                         axis=1) for s in range(seqs)], axis=0).astype(BF16)
    d = d_ref[0]
    pair = 2 * LANES
    for c in range(0, chunk * LANES, pair):
        y = jnp.dot(u[:, :c + pair], m_ref[0, :c + pair, c:c + pair], preferred_element_type=F32)
        y = y + jnp.dot(x, qc_ref[0, :, c:c + pair], preferred_element_type=F32)
        for j in (c // LANES, c // LANES + 1):
            yj = y[:, j * LANES - c:(j + 1) * LANES - c] + d * us[j]
            o_ref[pl.ds(j, rows, stride=chunk), :] = jax.nn.gelu(yj)


def _ssm_out(proj, x_prev, m, qc, d_skip, *, u_col0, batch, seqs):
    t = proj.shape[0]
    kt, cl, _ = m.shape
    _, n_slabs, n_rows, _ = x_prev.shape
    ns2 = n_slabs * LANES
    chunk = cl // LANES
    assert batch % seqs == 0 and chunk % 2 == 0
    tokens = t // batch * seqs
    return pl.pallas_call(
        functools.partial(_ssm_out_kernel, chunk=chunk, batch=batch, seqs=seqs),
        grid=(kt, batch // seqs),
        in_specs=[
            pl.BlockSpec((tokens, LANES), lambda k, i: (i, u_col0 + k)),
            pl.BlockSpec((1, n_slabs, n_rows, LANES), lambda k, i: (k, 0, 0, 0)),
            pl.BlockSpec((1, cl, cl), lambda k, i: (k, 0, 0)),
            pl.BlockSpec((1, ns2, cl), lambda k, i: (k, 0, 0)),
            pl.BlockSpec((1, 1, LANES), lambda k, i: (k, 0, 0)),
        ],
        out_specs=pl.BlockSpec((tokens, LANES), lambda k, i: (i, k)),
        out_shape=jax.ShapeDtypeStruct((t, kt * LANES), F32),
        compiler_params=_params("parallel", "arbitrary"),
        name="ssm_out",
    )(proj, x_prev, m, qc, d_skip.astype(F32).reshape(kt, 1, LANES))


def _outproj_kernel(h_ref, a_ref, y_ref, gw_ref, gb_ref, wa_ref, wy_ref, o_ref):
    y = y_ref[...]
    z = jnp.dot(y.astype(BF16), gw_ref[...], preferred_element_type=F32) + gb_ref[...]
    yg = (y * jax.nn.sigmoid(z)).astype(BF16)
    mixed = jnp.dot(a_ref[...], wa_ref[...], preferred_element_type=F32)
    mixed = mixed + jnp.dot(yg, wy_ref[...], preferred_element_type=F32)
    o_ref[...] = h_ref[...] + mixed


def _outproj(h, attn, y, glu_w, glu_b, w_out, *, tm=512):
    t, d = h.shape
    wa = attn.shape[1]
    wy = y.shape[1]
    tm = _tile(t, tm)
    const = lambda shape: pl.BlockSpec(shape, lambda i: (0, 0))
    return pl.pallas_call(
        _outproj_kernel,
        grid=(t // tm,),
        in_specs=[
            pl.BlockSpec((tm, d), lambda i: (i, 0)),
            pl.BlockSpec((tm, wa), lambda i: (i, 0)),
            pl.BlockSpec((tm, wy), lambda i: (i, 0)),
            const((wy, wy)), const((1, wy)),
            pl.BlockSpec((wa, d), lambda i: (0, 0)),
            pl.BlockSpec((wy, d), lambda i: (wa // wy, 0)),
        ],
        out_specs=pl.BlockSpec((tm, d), lambda i: (i, 0)),
        out_shape=jax.ShapeDtypeStruct((t, d), F32),
        compiler_params=_params("parallel"),
        name="outproj",
    )(h, attn, y, glu_w, glu_b.astype(F32).reshape(1, wy), w_out, w_out)


def kernel(x, ffn1_norm, ffn1_w_gate, ffn1_w_up, ffn1_w_down, mix_norm, w_in, q_norm, k_norm, rel_bias,
           ssm_lambda_re, ssm_lambda_im, ssm_log_dt, ssm_b_re, ssm_b_im, ssm_c_re, ssm_c_im, ssm_d,
           glu_w, glu_b, w_out, ffn2_norm, ffn2_w_gate, ffn2_w_up, ffn2_w_down):
    batch, seq, d_model = x.shape
    depth = w_in.shape[0]
    head_dim = q_norm.shape[-1]
    n_heads = rel_bias.shape[1]
    attn_width = n_heads * head_dim
    ssm_width = glu_w.shape[-1]
    assert w_in.shape[-1] == 3 * attn_width + ssm_width and attn_width % ssm_width == 0
    assert ssm_width % LANES == 0 and seq % SSM_CHUNK == 0
    u_col0 = 3 * attn_width // LANES

    bias = _bias_tables(rel_bias)
    h = x.reshape(batch * seq, d_model).astype(F32)
    for l in range(depth):
        h = _ffn(h, ffn1_norm[l], ffn1_w_gate[l].astype(BF16), ffn1_w_up[l].astype(BF16), ffn1_w_down[l])
        proj = _proj(h, mix_norm[l], w_in[l], q_norm[l], k_norm[l], attn_width=attn_width)
        attn = _attention(proj, bias, batch=batch, seq=seq, n_heads=n_heads, head_dim=head_dim)
        m, pe, qc, a_chunk = _ssm_prep(ssm_lambda_re[l], ssm_lambda_im[l], ssm_log_dt[l],
                                       ssm_b_re[l], ssm_b_im[l], ssm_c_re[l], ssm_c_im[l])
        e = _ssm_state(proj, pe, u_col0=u_col0, batch=batch)
        x_prev = _ssm_scan(e, a_chunk, batch=batch)
        y = _ssm_out(proj, x_prev, m, qc, ssm_d[l], u_col0=u_col0, batch=batch, seqs=2 if batch % 2 == 0 else 1)
        h = _outproj(h, attn, y, glu_w[l].astype(BF16), glu_b[l], w_out[l].astype(BF16))
        h = _ffn(h, ffn2_norm[l], ffn2_w_gate[l].astype(BF16), ffn2_w_up[l].astype(BF16), ffn2_w_down[l])
    return h.reshape(batch, seq, d_model).astype(x.dtype)
```

```python
import functools
import math

import jax
import jax.numpy as jnp
from jax import lax
from jax.experimental import pallas as pl
from jax.experimental.pallas import tpu as pltpu

F32 = jnp.float32
BF16 = jnp.bfloat16

EPS = 1e-6
NEG_INF = -1e30
DILATED_PATTERNS = ((128, 1), (512, 4), (2048, 16))
MAX_DISTANCE = 2048
SSM_CHUNK = 16
LANES = 128
VMEM_LIMIT_BYTES = 60 * 1024 * 1024


def _params(*semantics, flags=None):
    return pltpu.CompilerParams(dimension_semantics=semantics, vmem_limit_bytes=VMEM_LIMIT_BYTES, flags=flags)


def _tile(total, pref):
    if total <= pref:
        return total
    t = pref - pref % LANES
    while t > 0:
        if total % t == 0:
            return t
        t -= LANES
    return total


def _rms_norm(x, g):
    return x * lax.rsqrt(jnp.mean(x * x, axis=-1, keepdims=True) + EPS) * g


def _ffn_kernel(x_hbm, g_ref, wg_ref, wu_ref, wd_ref, o_ref, xn_ref, x_buf, x_sem):
    i, j = pl.program_id(0), pl.program_id(1)
    tm = x_buf.shape[0]

    def x_copy(tile):
        return pltpu.make_async_copy(x_hbm.at[pl.ds(tile * tm, tm), :], x_buf, x_sem)

    @pl.when((i == 0) & (j == 0))
    def _():
        x_copy(0).start()

    @pl.when(j == 0)
    def _():
        x_copy(i).wait()
        x = x_buf[...]
        xn_ref[...] = _rms_norm(x, g_ref[...]).astype(BF16)
        o_ref[...] = x

    @pl.when((j == 1) & (i + 1 < pl.num_programs(0)))
    def _():
        x_copy(i + 1).start()

    xn = xn_ref[...]
    a = jnp.dot(xn, wg_ref[...].astype(BF16), preferred_element_type=F32)
    b = jnp.dot(xn, wu_ref[...].astype(BF16), preferred_element_type=F32)
    h = (0.5 * a * jax.nn.sigmoid(a) * b).astype(BF16)
    o_ref[...] += jnp.dot(h, wd_ref[...].astype(BF16), preferred_element_type=F32)


def _ffn(x, gain, wg, wu, wd, *, tm=1024, tf=512):
    t, d = x.shape
    f = wg.shape[1]
    tm = _tile(t, tm)
    tf = _tile(f, tf)
    assert f // tf >= 2, "the x prefetch is issued at the second hidden tile"
    return pl.pallas_call(
        _ffn_kernel,
        grid=(t // tm, f // tf),
        in_specs=[
            pl.BlockSpec(memory_space=pl.ANY),
            pl.BlockSpec((1, d), lambda i, j: (0, 0)),
            pl.BlockSpec((d, tf), lambda i, j: (0, j)),
            pl.BlockSpec((d, tf), lambda i, j: (0, j)),
            pl.BlockSpec((tf, d), lambda i, j: (j, 0)),
        ],
        out_specs=pl.BlockSpec((tm, d), lambda i, j: (i, 0)),
        out_shape=jax.ShapeDtypeStruct((t, d), F32),
        scratch_shapes=[pltpu.VMEM((tm, d), BF16), pltpu.VMEM((tm, d), F32), pltpu.SemaphoreType.DMA],
        compiler_params=_params("arbitrary", "arbitrary"),
        name="ffn",
    )(x, gain.reshape(1, d), wg, wu, wd)


def _proj_kernel(h_ref, g_ref, w_ref, qkg_ref, o_ref, hn_ref, *, n_norm_tiles, head_dim):
    n = pl.program_id(1)

    @pl.when(n == 0)
    def _():
        hn_ref[...] = _rms_norm(h_ref[...], g_ref[...]).astype(BF16)

    acc = jnp.dot(hn_ref[...], w_ref[...].astype(BF16), preferred_element_type=F32)

    @pl.when(n < n_norm_tiles)
    def _():
        for c in range(0, acc.shape[1], head_dim):
            o_ref[:, c:c + head_dim] = _rms_norm(acc[:, c:c + head_dim], qkg_ref[0, :, c:c + head_dim])

    @pl.when(n >= n_norm_tiles)
    def _():
        o_ref[...] = acc


def _proj(h, gain, w_in, q_gain, k_gain, *, attn_width, tm=1024):
    t, d = h.shape
    n_out = w_in.shape[1]
    head_dim = q_gain.shape[0]
    tn = attn_width
    assert n_out % tn == 0
    n_tiles = n_out // tn
    reps = tn // head_dim
    qkg = jnp.ones((n_tiles, 1, tn), F32)
    qkg = qkg.at[0, 0].set(jnp.tile(q_gain, reps)).at[1, 0].set(jnp.tile(k_gain, reps))
    tm = _tile(t, tm)
    return pl.pallas_call(
        functools.partial(_proj_kernel, n_norm_tiles=2, head_dim=head_dim),
        grid=(t // tm, n_tiles),
        in_specs=[
            pl.BlockSpec((tm, d), lambda i, n: (i, 0)),
            pl.BlockSpec((1, d), lambda i, n: (0, 0)),
            pl.BlockSpec((d, tn), lambda i, n: (0, n)),
            pl.BlockSpec((1, 1, tn), lambda i, n: (n, 0, 0)),
        ],
        out_specs=pl.BlockSpec((tm, tn), lambda i, n: (i, n)),
        out_shape=jax.ShapeDtypeStruct((t, n_out), F32),
        scratch_shapes=[pltpu.VMEM((tm, d), BF16)],
        compiler_params=_params("parallel", "arbitrary"),
        name="proj",
    )(h, gain.reshape(1, d), w_in, qkg)


def _t5_bucket_tables(n_buckets):
    tables = []
    max_exact = n_buckets // 2
    for window, dilation in DILATED_PATTERNS:
        span = window // dilation
        qi = jnp.arange(span)[:, None]
        kj = jnp.arange(2 * span)[None, :]
        delta = qi + span - kj
        dist = jnp.maximum(delta, 0) * dilation
        d_f = jnp.maximum(dist, max_exact).astype(F32)
        large = max_exact + (jnp.log(d_f / max_exact) / math.log(MAX_DISTANCE / max_exact)
                             * (n_buckets - max_exact)).astype(jnp.int32)
        large = jnp.minimum(large, n_buckets - 1)
        bucket = jnp.where(dist < max_exact, dist, large)
        valid = (delta >= 0) & (delta <= span)
        tables.append(jnp.where(valid, bucket, -1).astype(jnp.int32))
    return jnp.stack(tables)


def _bias_kernel(rb_ref, bucket_ref, o_ref, *, n_buckets):
    h = pl.program_id(0)
    for p in range(bucket_ref.shape[0]):
        bk = bucket_ref[p]
        acc = jnp.full(bk.shape, NEG_INF, F32)
        for b in range(n_buckets):
            acc = jnp.where(bk == b, rb_ref[b, h] * math.log2(math.e), acc)
        o_ref[0, 2 * p] = acc
        in_prev_half = lax.broadcasted_iota(jnp.int32, bk.shape, 1) < bk.shape[1] // 2
        o_ref[0, 2 * p + 1] = jnp.where(in_prev_half, NEG_INF, acc)


def _bias_tables(rel_bias):
    n_buckets, n_heads = rel_bias.shape
    buckets = _t5_bucket_tables(n_buckets)
    n_pat, span, span2 = buckets.shape
    return pl.pallas_call(
        functools.partial(_bias_kernel, n_buckets=n_buckets),
        grid=(n_heads,),
        in_specs=[
            pl.BlockSpec(memory_space=pltpu.SMEM),
            pl.BlockSpec((n_pat, span, span2), lambda h: (0, 0, 0)),
        ],
        out_specs=pl.BlockSpec((1, 2 * n_pat, span, span2), lambda h: (h, 0, 0, 0)),
        out_shape=jax.ShapeDtypeStruct((n_heads, 2 * n_pat, span, span2), F32),
        compiler_params=_params("arbitrary"),
        name="bias",
    )(rel_bias.astype(F32), buckets)


ATTN_UNROLL = (32, 8, 8)
ATTN_STAGE_GROUP = 32


def _attn_kernel(q_ref, k_ref, v_ref, bias_ref, o_ref,
                 acc_a, m_a, l_a, acc_b, m_b, l_b, q4_ref, k4_ref, v4_ref, *, seq, span):
    hd = q_ref.shape[1]
    scale2 = math.log2(math.e) / math.sqrt(hd)
    nt_dims = (((1,), (1,)), ((), ()))
    g4, g16 = seq // 4, seq // 16
    state_a, state_b = (acc_a, m_a, l_a), (acc_b, m_b, l_b)
    src_nat, src_4 = (q_ref, k_ref, v_ref), (q4_ref, k4_ref, v4_ref)

    def run(n_classes, per_class, unroll, src, stride, pat, state_in, state_out, save4, start_of, out_of):
        assert per_class % unroll == 0 or unroll % per_class == 0
        total = n_classes * per_class
        assert total % unroll == 0

        def ld(ref, first_row):
            if stride == 1:
                return ref[pl.ds(first_row, span), :]
            return ref[pl.ds(first_row, span, stride=stride), :]

        def body(t, carry):
            blocks = []
            for u in range(unroll):
                if unroll % per_class == 0:
                    c, n = t * (unroll // per_class) + u // per_class, u % per_class
                    prev = "reuse" if n > 0 else None
                else:
                    trips_per_class = per_class // unroll
                    c, n = t // trips_per_class, (t % trips_per_class) * unroll + u
                    prev = "reuse" if u > 0 else "load"
                blocks.append(dict(n=n, prev=prev, start=start_of(c, n), rows_out=pl.ds(out_of(c, n), span)))

            for first in range(0, unroll, ATTN_STAGE_GROUP):
                group = blocks[first:first + ATTN_STAGE_GROUP]
                for u, blk in enumerate(group, first):
                    q32, k32, v32 = (ld(r, blk["start"]) for r in src)
                    if save4:
                        q4_ref[blk["rows_out"], :] = q32
                        k4_ref[blk["rows_out"], :] = k32
                        v4_ref[blk["rows_out"], :] = v32
                    blk["q"], blk["k"], blk["v"] = q32.astype(BF16), k32.astype(BF16), v32.astype(BF16)
                    if blk["prev"] is None:
                        blk["kk"], blk["vv"] = blk["k"], blk["v"]
                        blk["bias"] = (2 * pat, True)
                        continue
                    if blk["prev"] == "load":
                        has_prev = jnp.minimum(blk["n"], 1)
                        first_row = blk["start"] - span * stride * has_prev
                        k_prev, v_prev = ld(src[1], first_row).astype(BF16), ld(src[2], first_row).astype(BF16)
                        blk["bias"] = (2 * pat + 1 - has_prev, False)
                    else:
                        k_prev, v_prev = blocks[u - 1]["k"], blocks[u - 1]["v"]
                        blk["bias"] = (2 * pat, False)
                    blk["kk"] = jnp.concatenate([k_prev, blk["k"]], axis=0)
                    blk["vv"] = jnp.concatenate([v_prev, blk["v"]], axis=0)

                for blk in group:
                    blk["s"] = lax.dot_general(blk["q"], blk["kk"], nt_dims, preferred_element_type=F32)

                for blk in group:
                    table, cur_half_only = blk["bias"]
                    bias = bias_ref[0, table]
                    s = blk["s"] * scale2 + (bias[:, span:] if cur_half_only else bias)
                    m_blk = jnp.max(s, axis=-1, keepdims=True)
                    if state_in is None:
                        m_new = jnp.broadcast_to(m_blk, (span, hd))
                    else:
                        blk["m_old"] = ld(state_in[1], blk["start"])
                        m_new = jnp.maximum(blk["m_old"], m_blk)
                    m_wide = m_new if s.shape[1] == hd else jnp.concatenate([m_new, m_new], axis=1)
                    blk["p"] = jnp.exp2(s - m_wide).astype(BF16)
                    blk["m_new"] = m_new

                for blk in group:
                    v_aug = jnp.concatenate([blk["vv"], jnp.ones_like(blk["vv"])], axis=1)
                    blk["pv"] = jnp.dot(blk["p"], v_aug, preferred_element_type=F32)

                for blk in group:
                    o_new, l_new = blk["pv"][:, :hd], blk["pv"][:, hd:]
                    if state_in is not None:
                        alpha = jnp.exp2(blk["m_old"] - blk["m_new"])
                        o_new = alpha * ld(state_in[0], blk["start"]) + o_new
                        l_new = alpha * ld(state_in[2], blk["start"]) + l_new
                    state_out[0][blk["rows_out"], :] = o_new
                    state_out[1][blk["rows_out"], :] = blk["m_new"]
                    state_out[2][blk["rows_out"], :] = l_new
            return carry

        lax.fori_loop(0, total // unroll, body, 0)

    blk4 = 4 * span

    run(1, seq // span, ATTN_UNROLL[0], src_nat, 1, 0, None, state_a, False,
        lambda c, n: n * span, lambda c, n: n * span)

    run(4, g4 // span, ATTN_UNROLL[1], src_nat, 4, 1, state_a, state_b, True,
        lambda c, n: n * blk4 + c, lambda c, n: c * g4 + n * span)

    run(16, g16 // span, ATTN_UNROLL[2], src_4, 4, 2, state_b, state_a, False,
        lambda c, n: (c // 4) * g4 + n * blk4 + c % 4,
        lambda c, n: (c // 4) * g4 + (c % 4) * g16 + n * span)

    for c in range(16):
        rows = pl.ds((c // 4) * g4 + (c % 4) * g16, g16)
        acc_b[pl.ds((c // 4) * g4 + c % 4, g16, stride=4), :] = acc_a[rows, :] / l_a[rows, :]
    for b in range(4):
        acc_a[pl.ds(b, g4, stride=4), :] = acc_b[pl.ds(b * g4, g4), :]
    o_ref[...] = acc_a[...].astype(o_ref.dtype)


def _attention(proj, bias, *, batch, seq, n_heads, head_dim):
    assert DILATED_PATTERNS == ((128, 1), (512, 4), (2048, 16)), "row orders are built for dilations 1, 4, 16"
    span = 128
    assert seq % DILATED_PATTERNS[-1][0] == 0
    n_tables = bias.shape[1]
    qkv_spec = lambda off: pl.BlockSpec((seq, head_dim), lambda b, h: (b, off + h))
    return pl.pallas_call(
        functools.partial(_attn_kernel, seq=seq, span=span),
        grid=(batch, n_heads),
        in_specs=[
            qkv_spec(0), qkv_spec(n_heads), qkv_spec(2 * n_heads),
            pl.BlockSpec((1, n_tables, span, 2 * span), lambda b, h: (h, 0, 0, 0)),
        ],
        out_specs=pl.BlockSpec((seq, head_dim), lambda b, h: (b, h)),
        out_shape=jax.ShapeDtypeStruct((batch * seq, n_heads * head_dim), BF16),
        scratch_shapes=[pltpu.VMEM((seq, head_dim), F32)] * 9,
        compiler_params=_params("parallel", "arbitrary"),
        name="attn",
    )(proj, proj, proj, bias)


def _ssm_prep_kernel(lr_row_ref, li_row_ref, ldt_row_ref, bre_ref, bim_ref,
                     lr_col_ref, li_col_ref, ldt_col_ref, cre_ref, cim_ref,
                     mq_ref, pe_ref, a_ref, *, chunk, group, state):
    gpt = LANES // group
    ns = gpt * state

    def discretize(lr, li, ldt):
        dt = jnp.exp(ldt)
        mag = jnp.exp(lr * dt)
        return mag * jnp.cos(li * dt), mag * jnp.sin(li * dt)

    lr, li = lr_row_ref[0], li_row_ref[0]
    ar, ai = discretize(lr, li, ldt_row_ref[0])
    zr, zi = ar - 1.0, ai
    lam_sq = lr * lr + li * li
    coef_re = (zr * lr + zi * li) / lam_sq
    coef_im = (zi * lr - zr * li) / lam_sq
    bre, bim = bre_ref[0], bim_ref[0]
    bbar_re = coef_re * bre - coef_im * bim
    bbar_im = coef_re * bim + coef_im * bre

    row_g = lax.broadcasted_iota(jnp.int32, (LANES, ns), 0) // group
    col_g = lax.broadcasted_iota(jnp.int32, (LANES, ns), 1) // state
    mask_in = row_g == col_g

    def expand_in(w):
        tiled = jnp.concatenate([w] * gpt, axis=0)
        return jnp.where(mask_in, tiled, 0.0)

    pr = jnp.ones_like(ar)
    pi = jnp.zeros_like(ai)
    pe0 = None
    for m in range(chunk):
        w_re = pr * bbar_re - pi * bbar_im
        w_im = pr * bbar_im + pi * bbar_re
        blk = jnp.concatenate([expand_in(w_re), expand_in(w_im)], axis=1).astype(BF16)
        if m == 0:
            pe0 = blk
        j = chunk - 1 - m
        pe_ref[0, j * LANES:(j + 1) * LANES, :] = blk
        pr, pi = pr * ar - pi * ai, pr * ai + pi * ar
    a_ref[0] = jnp.concatenate([pr, pi], axis=1)

    acr, aci = discretize(lr_col_ref[0], li_col_ref[0], ldt_col_ref[0])
    cre, cim = cre_ref[0], cim_ref[0]
    row_g2 = lax.broadcasted_iota(jnp.int32, (ns, LANES), 0) // state
    col_g2 = lax.broadcasted_iota(jnp.int32, (ns, LANES), 1) // group
    mask_out = row_g2 == col_g2

    def expand_out(z):
        return jnp.where(mask_out, jnp.concatenate([z] * gpt, axis=0), 0.0)

    qr = jnp.ones_like(acr)
    qi = jnp.zeros_like(aci)
    for m in range(chunk + 1):
        z_re = expand_out(cre * qr - cim * qi)
        z_im = expand_out(-(cre * qi + cim * qr))
        qc_m = jnp.concatenate([z_re, z_im], axis=0).astype(BF16)
        if m >= 1:
            mq_ref[0, :2 * ns, (m - 1) * LANES:m * LANES] = qc_m
        if m < chunk:
            bd = jnp.dot(pe0, qc_m, preferred_element_type=F32).astype(BF16)
            for j in range(chunk - m):
                mq_ref[0, 2 * ns + j * LANES:2 * ns + (j + 1) * LANES, (j + m) * LANES:(j + m + 1) * LANES] = bd
            if m >= 1:
                zeros = jnp.zeros((LANES, LANES), BF16)
                for j in range(m, chunk):
                    mq_ref[0, 2 * ns + j * LANES:2 * ns + (j + 1) * LANES, (j - m) * LANES:(j - m + 1) * LANES] = zeros
        qr, qi = qr * acr - qi * aci, qr * aci + qi * acr


def _ssm_prep(lam_re, lam_im, log_dt, b_re, b_im, c_re, c_im):
    n_groups, state, group = b_re.shape
    gpt = LANES // group
    kt = n_groups // gpt
    ns = gpt * state
    chunk = SSM_CHUNK

    def row(a):
        return a.astype(F32).reshape(kt, 1, ns)

    def col(a):
        return jnp.repeat(jnp.transpose(a.astype(F32).reshape(kt, gpt, state), (0, 2, 1)), group, axis=-1)

    ldt = jnp.broadcast_to(log_dt.astype(F32)[:, None], (n_groups, state))
    b_cp = lambda b: jnp.transpose(b.astype(F32), (2, 0, 1)).reshape(group, kt, ns).transpose(1, 0, 2)
    c_pc = lambda c: jnp.transpose(c.astype(F32).reshape(kt, gpt, group, state), (0, 3, 1, 2)).reshape(kt, state, LANES)

    row_spec = pl.BlockSpec((1, 1, ns), lambda k: (k, 0, 0))
    b_spec = pl.BlockSpec((1, group, ns), lambda k: (k, 0, 0))
    col_spec = pl.BlockSpec((1, state, LANES), lambda k: (k, 0, 0))
    cl = chunk * LANES
    return pl.pallas_call(
        functools.partial(_ssm_prep_kernel, chunk=chunk, group=group, state=state),
        grid=(kt,),
        in_specs=[row_spec, row_spec, row_spec, b_spec, b_spec,
                  col_spec, col_spec, col_spec, col_spec, col_spec],
        out_specs=[
            pl.BlockSpec((1, 2 * ns + cl, cl), lambda k: (k, 0, 0)),
            pl.BlockSpec((1, cl, 2 * ns), lambda k: (k, 0, 0)),
            pl.BlockSpec((1, 1, 2 * ns), lambda k: (k, 0, 0)),
        ],
        out_shape=[
            jax.ShapeDtypeStruct((kt, 2 * ns + cl, cl), BF16),
            jax.ShapeDtypeStruct((kt, cl, 2 * ns), BF16),
            jax.ShapeDtypeStruct((kt, 1, 2 * ns), F32),
        ],
        compiler_params=_params("parallel"),
        name="ssm_prep",
    )(row(lam_re), row(lam_im), row(ldt), b_cp(b_re), b_cp(b_im),
      col(lam_re), col(lam_im), col(ldt), c_pc(c_re), c_pc(c_im))


def _chunk_rows(u_ref, rows, chunk):
    return [u_ref[pl.ds(j, rows, stride=chunk), :] for j in range(chunk)]


def _batch_rows(first, count, batch):
    return pl.ds(first, count) if batch == 1 else pl.ds(first, count, stride=batch)


def _ssm_state_kernel(u_ref, pe_ref, e_ref, *, chunk, batch):
    n_slabs, n_rows = e_ref.shape[1], e_ref.shape[2]
    n_chunks = n_rows // batch
    u = jnp.concatenate(_chunk_rows(u_ref, n_rows, chunk), axis=1).astype(BF16)
    e = jnp.dot(u, pe_ref[0], preferred_element_type=F32)
    for b in range(batch):
        for s in range(n_slabs):
            e_ref[0, s, _batch_rows(b, n_chunks, batch), :] = (
                e[b * n_chunks:(b + 1) * n_chunks, s * LANES:(s + 1) * LANES])


def _ssm_state(proj, pe, *, u_col0, batch):
    t = proj.shape[0]
    kt, cl, ns2 = pe.shape
    chunk = cl // LANES
    n_rows = t // chunk
    n_slabs = ns2 // LANES
    return pl.pallas_call(
        functools.partial(_ssm_state_kernel, chunk=chunk, batch=batch),
        grid=(kt,),
        in_specs=[
            pl.BlockSpec((t, LANES), lambda k: (0, u_col0 + k)),
            pl.BlockSpec((1, cl, ns2), lambda k: (k, 0, 0)),
        ],
        out_specs=pl.BlockSpec((1, n_slabs, n_rows, LANES), lambda k: (k, 0, 0, 0)),
        out_shape=jax.ShapeDtypeStruct((kt, n_slabs, n_rows, LANES), F32),
        compiler_params=_params("parallel"),
        name="ssm_state",
    )(proj, pe)


def _ssm_scan_kernel(e_ref, a_ref, x_ref, *, batch):
    n_slabs, n_rows = e_ref.shape[1], e_ref.shape[2]
    half = n_slabs // 2
    n_chunks = n_rows // batch
    ar = jnp.broadcast_to(a_ref[0, :half], (half, batch, LANES))
    ai = jnp.broadcast_to(a_ref[0, half:], (half, batch, LANES))

    def step(n, carry):
        xr, xi = carry
        rows = pl.ds(n * batch, batch)
        x_ref[0, :half, rows, :] = xr
        x_ref[0, half:, rows, :] = xi
        er = e_ref[0, :half, rows, :]
        ei = e_ref[0, half:, rows, :]
        return ar * xr - ai * xi + er, ar * xi + ai * xr + ei

    zero = jnp.zeros((half, batch, LANES), F32)
    lax.fori_loop(0, n_chunks, step, (zero, zero))


def _ssm_scan(e, a_chunk, *, batch):
    kt, n_slabs, n_rows, _ = e.shape
    spec = pl.BlockSpec((1, n_slabs, n_rows, LANES), lambda k: (k, 0, 0, 0))
    return pl.pallas_call(
        functools.partial(_ssm_scan_kernel, batch=batch),
        grid=(kt,),
        in_specs=[spec, pl.BlockSpec((1, n_slabs, 1, LANES), lambda k: (k, 0, 0, 0))],
        out_specs=spec,
        out_shape=jax.ShapeDtypeStruct(e.shape, F32),
        compiler_params=_params("parallel"),
        name="ssm_scan",
    )(e, a_chunk.reshape(kt, n_slabs, 1, LANES))


def _ssm_out_kernel(u_ref, x_ref, mq_ref, d_ref, o_ref, *, chunk, batch, seqs):
    i = pl.program_id(1)
    rows = u_ref.shape[0] // chunk
    n_chunks = rows // seqs
    n_slabs = x_ref.shape[1]
    ns2 = n_slabs * LANES
    us = _chunk_rows(u_ref, rows, chunk)
    x = jnp.concatenate(
        [jnp.concatenate([x_ref[0, sl, _batch_rows(i * seqs + s, n_chunks, batch), :] for sl in range(n_slabs)],
                         axis=1) for s in range(seqs)], axis=0)
    xu = jnp.concatenate([x] + us, axis=1).astype(BF16)
    d = d_ref[0]
    pair = 2 * LANES
    for c in range(0, chunk * LANES, pair):
        k_rows = ns2 + c + pair
        y = jnp.dot(xu[:, :k_rows], mq_ref[0, :k_rows, c:c + pair], preferred_element_type=F32)
        for j in (c // LANES, c // LANES + 1):
            yj = y[:, j * LANES - c:(j + 1) * LANES - c] + d * us[j]
            o_ref[pl.ds(j, rows, stride=chunk), :] = jax.nn.gelu(yj)


def _ssm_out(proj, x_prev, mq, d_skip, *, u_col0, batch, seqs):
    t = proj.shape[0]
    _, n_slabs, n_rows, _ = x_prev.shape
    ns2 = n_slabs * LANES
    kt, _, cl = mq.shape
    chunk = cl // LANES
    assert batch % seqs == 0 and chunk % 2 == 0
    tokens = t // batch * seqs
    return pl.pallas_call(
        functools.partial(_ssm_out_kernel, chunk=chunk, batch=batch, seqs=seqs),
        grid=(kt, batch // seqs),
        in_specs=[
            pl.BlockSpec((tokens, LANES), lambda k, i: (i, u_col0 + k)),
            pl.BlockSpec((1, n_slabs, n_rows, LANES), lambda k, i: (k, 0, 0, 0)),
            pl.BlockSpec((1, ns2 + cl, cl), lambda k, i: (k, 0, 0)),
            pl.BlockSpec((1, 1, LANES), lambda k, i: (k, 0, 0)),
        ],
        out_specs=pl.BlockSpec((tokens, LANES), lambda k, i: (i, k)),
        out_shape=jax.ShapeDtypeStruct((t, kt * LANES), F32),
        compiler_params=_params("parallel", "arbitrary"),
        name="ssm_out",
    )(proj, x_prev, mq, d_skip.astype(F32).reshape(kt, 1, LANES))


def _outproj_kernel(h_ref, a_ref, y_ref, gw_ref, gb_ref, wa_ref, wy_ref, o_ref):
    y = y_ref[...]
    z = jnp.dot(y.astype(BF16), gw_ref[...], preferred_element_type=F32) + gb_ref[...]
    yg = (y * jax.nn.sigmoid(z)).astype(BF16)
    mixed = jnp.dot(a_ref[...], wa_ref[...], preferred_element_type=F32)
    mixed = mixed + jnp.dot(yg, wy_ref[...], preferred_element_type=F32)
    o_ref[...] = h_ref[...] + mixed


def _outproj(h, attn, y, glu_w, glu_b, w_out, *, tm=512):
    t, d = h.shape
    wa = attn.shape[1]
    wy = y.shape[1]
    tm = _tile(t, tm)
    const = lambda shape: pl.BlockSpec(shape, lambda i: (0, 0))
    return pl.pallas_call(
        _outproj_kernel,
        grid=(t // tm,),
        in_specs=[
            pl.BlockSpec((tm, d), lambda i: (i, 0)),
            pl.BlockSpec((tm, wa), lambda i: (i, 0)),
            pl.BlockSpec((tm, wy), lambda i: (i, 0)),
            const((wy, wy)), const((1, wy)),
            pl.BlockSpec((wa, d), lambda i: (0, 0)),
            pl.BlockSpec((wy, d), lambda i: (wa // wy, 0)),
        ],
        out_specs=pl.BlockSpec((tm, d), lambda i: (i, 0)),
        out_shape=jax.ShapeDtypeStruct((t, d), F32),
        compiler_params=_params("parallel"),
        name="outproj",
    )(h, attn, y, glu_w, glu_b.astype(F32).reshape(1, wy), w_out, w_out)


def kernel(x, ffn1_norm, ffn1_w_gate, ffn1_w_up, ffn1_w_down, mix_norm, w_in, q_norm, k_norm, rel_bias,
           ssm_lambda_re, ssm_lambda_im, ssm_log_dt, ssm_b_re, ssm_b_im, ssm_c_re, ssm_c_im, ssm_d,
           glu_w, glu_b, w_out, ffn2_norm, ffn2_w_gate, ffn2_w_up, ffn2_w_down):
    batch, seq, d_model = x.shape
    depth = w_in.shape[0]
    head_dim = q_norm.shape[-1]
    n_heads = rel_bias.shape[1]
    attn_width = n_heads * head_dim
    ssm_width = glu_w.shape[-1]
    assert w_in.shape[-1] == 3 * attn_width + ssm_width and attn_width % ssm_width == 0
    assert ssm_width % LANES == 0 and seq % SSM_CHUNK == 0
    u_col0 = 3 * attn_width // LANES

    bias = _bias_tables(rel_bias)
    h = x.reshape(batch * seq, d_model).astype(F32)
    for l in range(depth):
        h = _ffn(h, ffn1_norm[l], ffn1_w_gate[l], ffn1_w_up[l], ffn1_w_down[l])
        proj = _proj(h, mix_norm[l], w_in[l].astype(BF16), q_norm[l], k_norm[l], attn_width=attn_width)
        attn = _attention(proj, bias, batch=batch, seq=seq, n_heads=n_heads, head_dim=head_dim)
        mq, pe, a_chunk = _ssm_prep(ssm_lambda_re[l], ssm_lambda_im[l], ssm_log_dt[l],
                                    ssm_b_re[l], ssm_b_im[l], ssm_c_re[l], ssm_c_im[l])
        e = _ssm_state(proj, pe, u_col0=u_col0, batch=batch)
        x_prev = _ssm_scan(e, a_chunk, batch=batch)
        y = _ssm_out(proj, x_prev, mq, ssm_d[l], u_col0=u_col0, batch=batch, seqs=2 if batch % 2 == 0 else 1)
        h = _outproj(h, attn, y, glu_w[l].astype(BF16), glu_b[l], w_out[l].astype(BF16))
        h = _ffn(h, ffn2_norm[l], ffn2_w_gate[l], ffn2_w_up[l], ffn2_w_down[l])
    return h.reshape(batch, seq, d_model).astype(x.dtype)
```

```python
import functools
import math

import jax
import jax.numpy as jnp
from jax import lax
from jax.experimental import pallas as pl
from jax.experimental.pallas import tpu as pltpu

F32 = jnp.float32
BF16 = jnp.bfloat16

EPS = 1e-6
NEG_INF = -1e30
DILATED_PATTERNS = ((128, 1), (512, 4), (2048, 16))
MAX_DISTANCE = 2048
SSM_CHUNK = 16
LANES = 128
VMEM_LIMIT_BYTES = 60 * 1024 * 1024


def _params(*semantics, flags=None):
    return pltpu.CompilerParams(dimension_semantics=semantics, vmem_limit_bytes=VMEM_LIMIT_BYTES, flags=flags)


def _tile(total, pref):
    if total <= pref:
        return total
    t = pref - pref % LANES
    while t > 0:
        if total % t == 0:
            return t
        t -= LANES
    return total


def _rms_norm(x, g):
    return x * lax.rsqrt(jnp.mean(x * x, axis=-1, keepdims=True) + EPS) * g


def _ffn_kernel(x_hbm, g_ref, wg_ref, wu_ref, wd_ref, o_ref, xn_ref, x_buf, x_sem):
    i, j = pl.program_id(0), pl.program_id(1)
    tm = x_buf.shape[0]

    def x_copy(tile):
        return pltpu.make_async_copy(x_hbm.at[pl.ds(tile * tm, tm), :], x_buf, x_sem)

    @pl.when((i == 0) & (j == 0))
    def _():
        x_copy(0).start()

    @pl.when(j == 0)
    def _():
        x_copy(i).wait()
        x = x_buf[...]
        xn_ref[...] = _rms_norm(x, g_ref[...]).astype(BF16)
        o_ref[...] = x

    @pl.when((j == 1) & (i + 1 < pl.num_programs(0)))
    def _():
        x_copy(i + 1).start()

    xn = xn_ref[...]
    a = jnp.dot(xn, wg_ref[...].astype(BF16), preferred_element_type=F32)
    b = jnp.dot(xn, wu_ref[...].astype(BF16), preferred_element_type=F32)
    h = (0.5 * a * jax.nn.sigmoid(a) * b).astype(BF16)
    o_ref[...] += jnp.dot(h, wd_ref[...].astype(BF16), preferred_element_type=F32)


def _ffn(x, gain, wg, wu, wd, *, tm=1024, tf=512):
    t, d = x.shape
    f = wg.shape[1]
    tm = _tile(t, tm)
    tf = _tile(f, tf)
    assert f // tf >= 2, "the x prefetch is issued at the second hidden tile"
    return pl.pallas_call(
        _ffn_kernel,
        grid=(t // tm, f // tf),
        in_specs=[
            pl.BlockSpec(memory_space=pl.ANY),
            pl.BlockSpec((1, d), lambda i, j: (0, 0)),
            pl.BlockSpec((d, tf), lambda i, j: (0, j)),
            pl.BlockSpec((d, tf), lambda i, j: (0, j)),
            pl.BlockSpec((tf, d), lambda i, j: (j, 0)),
        ],
        out_specs=pl.BlockSpec((tm, d), lambda i, j: (i, 0)),
        out_shape=jax.ShapeDtypeStruct((t, d), F32),
        scratch_shapes=[pltpu.VMEM((tm, d), BF16), pltpu.VMEM((tm, d), F32), pltpu.SemaphoreType.DMA],
        compiler_params=_params("arbitrary", "arbitrary"),
        name="ffn",
    )(x, gain.reshape(1, d), wg, wu, wd)


def _proj_kernel(h_ref, g_ref, w_ref, qkg_ref, o_ref, hn_ref, *, n_norm_tiles, head_dim):
    n = pl.program_id(1)

    @pl.when(n == 0)
    def _():
        hn_ref[...] = _rms_norm(h_ref[...], g_ref[...]).astype(BF16)

    acc = jnp.dot(hn_ref[...], w_ref[...].astype(BF16), preferred_element_type=F32)

    @pl.when(n < n_norm_tiles)
    def _():
        for c in range(0, acc.shape[1], head_dim):
            o_ref[:, c:c + head_dim] = _rms_norm(acc[:, c:c + head_dim], qkg_ref[0, :, c:c + head_dim])

    @pl.when(n >= n_norm_tiles)
    def _():
        o_ref[...] = acc


def _proj(h, gain, w_in, q_gain, k_gain, *, attn_width, tm=1024):
    t, d = h.shape
    n_out = w_in.shape[1]
    head_dim = q_gain.shape[0]
    tn = attn_width
    assert n_out % tn == 0
    n_tiles = n_out // tn
    reps = tn // head_dim
    qkg = jnp.ones((n_tiles, 1, tn), F32)
    qkg = qkg.at[0, 0].set(jnp.tile(q_gain, reps)).at[1, 0].set(jnp.tile(k_gain, reps))
    tm = _tile(t, tm)
    return pl.pallas_call(
        functools.partial(_proj_kernel, n_norm_tiles=2, head_dim=head_dim),
        grid=(t // tm, n_tiles),
        in_specs=[
            pl.BlockSpec((tm, d), lambda i, n: (i, 0)),
            pl.BlockSpec((1, d), lambda i, n: (0, 0)),
            pl.BlockSpec((d, tn), lambda i, n: (0, n)),
            pl.BlockSpec((1, 1, tn), lambda i, n: (n, 0, 0)),
        ],
        out_specs=pl.BlockSpec((tm, tn), lambda i, n: (i, n)),
        out_shape=jax.ShapeDtypeStruct((t, n_out), F32),
        scratch_shapes=[pltpu.VMEM((tm, d), BF16)],
        compiler_params=_params("parallel", "arbitrary"),
        name="proj",
    )(h, gain.reshape(1, d), w_in, qkg)


def _t5_bucket_tables(n_buckets):
    tables = []
    max_exact = n_buckets // 2
    for window, dilation in DILATED_PATTERNS:
        span = window // dilation
        qi = jnp.arange(span)[:, None]
        kj = jnp.arange(2 * span)[None, :]
        delta = qi + span - kj
        dist = jnp.maximum(delta, 0) * dilation
        d_f = jnp.maximum(dist, max_exact).astype(F32)
        large = max_exact + (jnp.log(d_f / max_exact) / math.log(MAX_DISTANCE / max_exact)
                             * (n_buckets - max_exact)).astype(jnp.int32)
        large = jnp.minimum(large, n_buckets - 1)
        bucket = jnp.where(dist < max_exact, dist, large)
        valid = (delta >= 0) & (delta <= span)
        tables.append(jnp.where(valid, bucket, -1).astype(jnp.int32))
    return jnp.stack(tables)


def _bias_kernel(rb_ref, bucket_ref, o_ref, *, n_buckets):
    h = pl.program_id(0)
    for p in range(bucket_ref.shape[0]):
        bk = bucket_ref[p]
        acc = jnp.full(bk.shape, NEG_INF, F32)
        for b in range(n_buckets):
            acc = jnp.where(bk == b, rb_ref[b, h] * math.log2(math.e), acc)
        o_ref[0, 2 * p] = acc
        in_prev_half = lax.broadcasted_iota(jnp.int32, bk.shape, 1) < bk.shape[1] // 2
        o_ref[0, 2 * p + 1] = jnp.where(in_prev_half, NEG_INF, acc)


def _bias_tables(rel_bias):
    n_buckets, n_heads = rel_bias.shape
    buckets = _t5_bucket_tables(n_buckets)
    n_pat, span, span2 = buckets.shape
    return pl.pallas_call(
        functools.partial(_bias_kernel, n_buckets=n_buckets),
        grid=(n_heads,),
        in_specs=[
            pl.BlockSpec(memory_space=pltpu.SMEM),
            pl.BlockSpec((n_pat, span, span2), lambda h: (0, 0, 0)),
        ],
        out_specs=pl.BlockSpec((1, 2 * n_pat, span, span2), lambda h: (h, 0, 0, 0)),
        out_shape=jax.ShapeDtypeStruct((n_heads, 2 * n_pat, span, span2), F32),
        compiler_params=_params("arbitrary"),
        name="bias",
    )(rel_bias.astype(F32), buckets)


ATTN_UNROLL = (32, 8, 8)


def _attn_kernel(q_ref, k_ref, v_ref, bias_ref, o_ref,
                 acc_a, m_a, l_a, acc_b, m_b, l_b, q4_ref, k4_ref, v4_ref, *, seq, span):
    hd = q_ref.shape[1]
    scale2 = math.log2(math.e) / math.sqrt(hd)
    nt_dims = (((1,), (1,)), ((), ()))
    g4, g16 = seq // 4, seq // 16
    state_a, state_b = (acc_a, m_a, l_a), (acc_b, m_b, l_b)
    src_nat, src_4 = (q_ref, k_ref, v_ref), (q4_ref, k4_ref, v4_ref)

    def run(n_classes, per_class, unroll, src, stride, pat, state_in, state_out, save4, start_of, out_of):
        assert per_class % unroll == 0 or unroll % per_class == 0
        total = n_classes * per_class
        assert total % unroll == 0

        def ld(ref, first_row):
            if stride == 1:
                return ref[pl.ds(first_row, span), :]
            return ref[pl.ds(first_row, span, stride=stride), :]

        def body(t, carry):
            blocks = []
            for u in range(unroll):
                if unroll % per_class == 0:
                    c, n = t * (unroll // per_class) + u // per_class, u % per_class
                    prev = "reuse" if n > 0 else None
                else:
                    trips_per_class = per_class // unroll
                    c, n = t // trips_per_class, (t % trips_per_class) * unroll + u
                    prev = "reuse" if u > 0 else "load"
                blocks.append(dict(n=n, prev=prev, start=start_of(c, n), rows_out=pl.ds(out_of(c, n), span)))

            for u, blk in enumerate(blocks):
                q32, k32, v32 = (ld(r, blk["start"]) for r in src)
                if save4:
                    q4_ref[blk["rows_out"], :] = q32
                    k4_ref[blk["rows_out"], :] = k32
                    v4_ref[blk["rows_out"], :] = v32
                blk["q"], blk["k"], blk["v"] = q32.astype(BF16), k32.astype(BF16), v32.astype(BF16)
                if blk["prev"] is None:
                    blk["kk"], blk["vv"] = blk["k"], blk["v"]
                    blk["bias"] = (2 * pat, True)
                    continue
                if blk["prev"] == "load":
                    has_prev = jnp.minimum(blk["n"], 1)
                    first_row = blk["start"] - span * stride * has_prev
                    k_prev, v_prev = ld(src[1], first_row).astype(BF16), ld(src[2], first_row).astype(BF16)
                    blk["bias"] = (2 * pat + 1 - has_prev, False)
                else:
                    k_prev, v_prev = blocks[u - 1]["k"], blocks[u - 1]["v"]
                    blk["bias"] = (2 * pat, False)
                blk["kk"] = jnp.concatenate([k_prev, blk["k"]], axis=0)
                blk["vv"] = jnp.concatenate([v_prev, blk["v"]], axis=0)

            for blk in blocks:
                blk["s"] = lax.dot_general(blk["q"], blk["kk"], nt_dims, preferred_element_type=F32)

            for blk in blocks:
                table, cur_half_only = blk["bias"]
                bias = bias_ref[0, table]
                s = blk["s"] * scale2 + (bias[:, span:] if cur_half_only else bias)
                m_blk = jnp.max(s, axis=-1, keepdims=True)
                if state_in is None:
                    m_new = jnp.broadcast_to(m_blk, (span, hd))
                else:
                    blk["m_old"] = ld(state_in[1], blk["start"])
                    m_new = jnp.maximum(blk["m_old"], m_blk)
                m_wide = m_new if s.shape[1] == hd else jnp.concatenate([m_new, m_new], axis=1)
                blk["p"] = jnp.exp2(s - m_wide).astype(BF16)
                blk["m_new"] = m_new

            for blk in blocks:
                v_aug = jnp.concatenate([blk["vv"], jnp.ones_like(blk["vv"])], axis=1)
                blk["pv"] = jnp.dot(blk["p"], v_aug, preferred_element_type=F32)

            for blk in blocks:
                o_new, l_new = blk["pv"][:, :hd], blk["pv"][:, hd:]
                if state_in is not None:
                    alpha = jnp.exp2(blk["m_old"] - blk["m_new"])
                    o_new = alpha * ld(state_in[0], blk["start"]) + o_new
                    l_new = alpha * ld(state_in[2], blk["start"]) + l_new
                state_out[0][blk["rows_out"], :] = o_new
                state_out[1][blk["rows_out"], :] = blk["m_new"]
                state_out[2][blk["rows_out"], :] = l_new
            return carry

        lax.fori_loop(0, total // unroll, body, 0)

    blk4 = 4 * span

    run(1, seq // span, ATTN_UNROLL[0], src_nat, 1, 0, None, state_a, False,
        lambda c, n: n * span, lambda c, n: n * span)

    run(4, g4 // span, ATTN_UNROLL[1], src_nat, 4, 1, state_a, state_b, True,
        lambda c, n: n * blk4 + c, lambda c, n: c * g4 + n * span)

    run(16, g16 // span, ATTN_UNROLL[2], src_4, 4, 2, state_b, state_a, False,
        lambda c, n: (c // 4) * g4 + n * blk4 + c % 4,
        lambda c, n: (c // 4) * g4 + (c % 4) * g16 + n * span)

    for c in range(16):
        rows = pl.ds((c // 4) * g4 + (c % 4) * g16, g16)
        acc_b[pl.ds((c // 4) * g4 + c % 4, g16, stride=4), :] = acc_a[rows, :] / l_a[rows, :]
    for b in range(4):
        acc_a[pl.ds(b, g4, stride=4), :] = acc_b[pl.ds(b * g4, g4), :]
    o_ref[...] = acc_a[...].astype(o_ref.dtype)


def _attention(proj, bias, *, batch, seq, n_heads, head_dim):
    assert DILATED_PATTERNS == ((128, 1), (512, 4), (2048, 16)), "row orders are built for dilations 1, 4, 16"
    span = 128
    assert seq % DILATED_PATTERNS[-1][0] == 0
    n_tables = bias.shape[1]
    qkv_spec = lambda off: pl.BlockSpec((seq, head_dim), lambda b, h: (b, off + h))
    return pl.pallas_call(
        functools.partial(_attn_kernel, seq=seq, span=span),
        grid=(batch, n_heads),
        in_specs=[
            qkv_spec(0), qkv_spec(n_heads), qkv_spec(2 * n_heads),
            pl.BlockSpec((1, n_tables, span, 2 * span), lambda b, h: (h, 0, 0, 0)),
        ],
        out_specs=pl.BlockSpec((seq, head_dim), lambda b, h: (b, h)),
        out_shape=jax.ShapeDtypeStruct((batch * seq, n_heads * head_dim), BF16),
        scratch_shapes=[pltpu.VMEM((seq, head_dim), F32)] * 9,
        compiler_params=_params("parallel", "arbitrary"),
        name="attn",
    )(proj, proj, proj, bias)


def _ssm_discretize(lr, li, ldt):
    dt = jnp.exp(ldt)
    mag = jnp.exp(lr * dt)
    return mag * jnp.cos(li * dt), mag * jnp.sin(li * dt)


def _ssm_in_blocks(in_refs, n_powers, store, *, group, state):
    lr_ref, li_ref, ldt_ref, bre_ref, bim_ref = in_refs
    gpt = LANES // group
    ns = gpt * state
    lr, li = lr_ref[0], li_ref[0]
    ar, ai = _ssm_discretize(lr, li, ldt_ref[0])
    zr, zi = ar - 1.0, ai
    lam_sq = lr * lr + li * li
    coef_re = (zr * lr + zi * li) / lam_sq
    coef_im = (zi * lr - zr * li) / lam_sq
    bre, bim = bre_ref[0], bim_ref[0]
    bbar_re = coef_re * bre - coef_im * bim
    bbar_im = coef_re * bim + coef_im * bre

    row_g = lax.broadcasted_iota(jnp.int32, (LANES, ns), 0) // group
    col_g = lax.broadcasted_iota(jnp.int32, (LANES, ns), 1) // state
    mask_in = row_g == col_g

    def expand_in(w):
        return jnp.where(mask_in, jnp.concatenate([w] * gpt, axis=0), 0.0)

    pr = jnp.ones_like(ar)
    pi = jnp.zeros_like(ai)
    for m in range(n_powers):
        w_re = pr * bbar_re - pi * bbar_im
        w_im = pr * bbar_im + pi * bbar_re
        store(m, jnp.concatenate([expand_in(w_re), expand_in(w_im)], axis=1).astype(BF16))
        pr, pi = pr * ar - pi * ai, pr * ai + pi * ar
    return pr, pi


def _ssm_out_matrix(in_refs, out_refs, mq_ref, *, chunk, group, state):
    lr_ref, li_ref, ldt_ref, cre_ref, cim_ref = out_refs
    gpt = LANES // group
    ns = gpt * state
    first = []
    _ssm_in_blocks(in_refs, 1, lambda m, blk: first.append(blk), group=group, state=state)
    pe0 = first[0]

    acr, aci = _ssm_discretize(lr_ref[0], li_ref[0], ldt_ref[0])
    cre, cim = cre_ref[0], cim_ref[0]
    row_g = lax.broadcasted_iota(jnp.int32, (ns, LANES), 0) // state
    col_g = lax.broadcasted_iota(jnp.int32, (ns, LANES), 1) // group
    mask_out = row_g == col_g

    def expand_out(z):
        return jnp.where(mask_out, jnp.concatenate([z] * gpt, axis=0), 0.0)

    qr = jnp.ones_like(acr)
    qi = jnp.zeros_like(aci)
    zeros = jnp.zeros((LANES, LANES), BF16)
    for m in range(chunk + 1):
        z_re = expand_out(cre * qr - cim * qi)
        z_im = expand_out(-(cre * qi + cim * qr))
        qc_m = jnp.concatenate([z_re, z_im], axis=0).astype(BF16)
        if m >= 1:
            mq_ref[:2 * ns, (m - 1) * LANES:m * LANES] = qc_m
        if m < chunk:
            bd = jnp.dot(pe0, qc_m, preferred_element_type=F32).astype(BF16)
            for j in range(chunk - m):
                mq_ref[2 * ns + j * LANES:2 * ns + (j + 1) * LANES, (j + m) * LANES:(j + m + 1) * LANES] = bd
            if m >= 1:
                for j in range(m, chunk):
                    mq_ref[2 * ns + j * LANES:2 * ns + (j + 1) * LANES, (j - m) * LANES:(j - m + 1) * LANES] = zeros
        qr, qi = qr * acr - qi * aci, qr * aci + qi * acr


def _ssm_params(lam_re, lam_im, log_dt, b_re, b_im, c_re, c_im):
    n_groups, state, group = b_re.shape
    gpt = LANES // group
    kt = n_groups // gpt
    ns = gpt * state

    def row(a):
        return a.astype(F32).reshape(kt, 1, ns)

    def col(a):
        return jnp.repeat(jnp.transpose(a.astype(F32).reshape(kt, gpt, state), (0, 2, 1)), group, axis=-1)

    ldt = jnp.broadcast_to(log_dt.astype(F32)[:, None], (n_groups, state))
    b_cp = lambda b: jnp.transpose(b.astype(F32), (2, 0, 1)).reshape(group, kt, ns).transpose(1, 0, 2)
    c_pc = lambda c: jnp.transpose(c.astype(F32).reshape(kt, gpt, group, state), (0, 3, 1, 2)).reshape(kt, state, LANES)

    in_arrays = (row(lam_re), row(lam_im), row(ldt), b_cp(b_re), b_cp(b_im))
    out_arrays = (col(lam_re), col(lam_im), col(ldt), c_pc(c_re), c_pc(c_im))

    def in_specs(tile_of):
        row_spec = pl.BlockSpec((1, 1, ns), lambda *g: (tile_of(*g), 0, 0))
        b_spec = pl.BlockSpec((1, group, ns), lambda *g: (tile_of(*g), 0, 0))
        return [row_spec, row_spec, row_spec, b_spec, b_spec]

    def out_specs(tile_of):
        return [pl.BlockSpec((1, state, LANES), lambda *g: (tile_of(*g), 0, 0))] * 5

    dims = dict(kt=kt, ns=ns, group=group, state=state, chunk=SSM_CHUNK)
    return in_arrays, out_arrays, in_specs, out_specs, dims


def _chunk_rows(u_ref, rows, chunk):
    return [u_ref[pl.ds(j, rows, stride=chunk), :] for j in range(chunk)]


def _batch_rows(first, count, batch):
    return pl.ds(first, count) if batch == 1 else pl.ds(first, count, stride=batch)


def _ssm_state_kernel(u_ref, lr_ref, li_ref, ldt_ref, bre_ref, bim_ref, e_ref, a_ref, pe_ref, *,
                      chunk, batch, group, state):
    n_slabs, n_rows = e_ref.shape[1], e_ref.shape[2]
    n_chunks = n_rows // batch

    def store_pe(m, blk):
        j = chunk - 1 - m
        pe_ref[j * LANES:(j + 1) * LANES, :] = blk

    pr, pi = _ssm_in_blocks((lr_ref, li_ref, ldt_ref, bre_ref, bim_ref), chunk, store_pe,
                            group=group, state=state)
    a_ref[0] = jnp.concatenate([pr, pi], axis=1)

    u = jnp.concatenate(_chunk_rows(u_ref, n_rows, chunk), axis=1).astype(BF16)
    e = jnp.dot(u, pe_ref[...], preferred_element_type=F32)
    for b in range(batch):
        for s in range(n_slabs):
            e_ref[0, s, _batch_rows(b, n_chunks, batch), :] = (
                e[b * n_chunks:(b + 1) * n_chunks, s * LANES:(s + 1) * LANES])


def _ssm_state(proj, in_arrays, in_specs, dims, *, u_col0, batch):
    t = proj.shape[0]
    kt, ns, chunk = dims["kt"], dims["ns"], dims["chunk"]
    n_rows = t // chunk
    n_slabs = 2 * ns // LANES
    return pl.pallas_call(
        functools.partial(_ssm_state_kernel, chunk=chunk, batch=batch, group=dims["group"], state=dims["state"]),
        grid=(kt,),
        in_specs=[pl.BlockSpec((t, LANES), lambda k: (0, u_col0 + k))] + in_specs(lambda k: k),
        out_specs=[
            pl.BlockSpec((1, n_slabs, n_rows, LANES), lambda k: (k, 0, 0, 0)),
            pl.BlockSpec((1, 1, 2 * ns), lambda k: (k, 0, 0)),
        ],
        out_shape=[
            jax.ShapeDtypeStruct((kt, n_slabs, n_rows, LANES), F32),
            jax.ShapeDtypeStruct((kt, 1, 2 * ns), F32),
        ],
        scratch_shapes=[pltpu.VMEM((chunk * LANES, 2 * ns), BF16)],
        compiler_params=_params("parallel"),
        name="ssm_state",
    )(proj, *in_arrays)


def _ssm_scan_kernel(e_ref, a_ref, x_ref, *, batch):
    n_slabs, n_rows = e_ref.shape[1], e_ref.shape[2]
    half = n_slabs // 2
    n_chunks = n_rows // batch
    ar = jnp.broadcast_to(a_ref[0, :half], (half, batch, LANES))
    ai = jnp.broadcast_to(a_ref[0, half:], (half, batch, LANES))

    def step(n, carry):
        xr, xi = carry
        rows = pl.ds(n * batch, batch)
        x_ref[0, :half, rows, :] = xr
        x_ref[0, half:, rows, :] = xi
        er = e_ref[0, :half, rows, :]
        ei = e_ref[0, half:, rows, :]
        return ar * xr - ai * xi + er, ar * xi + ai * xr + ei

    zero = jnp.zeros((half, batch, LANES), F32)
    lax.fori_loop(0, n_chunks, step, (zero, zero))


def _ssm_scan(e, a_chunk, *, batch):
    kt, n_slabs, n_rows, _ = e.shape
    spec = pl.BlockSpec((1, n_slabs, n_rows, LANES), lambda k: (k, 0, 0, 0))
    return pl.pallas_call(
        functools.partial(_ssm_scan_kernel, batch=batch),
        grid=(kt,),
        in_specs=[spec, pl.BlockSpec((1, n_slabs, 1, LANES), lambda k: (k, 0, 0, 0))],
        out_specs=spec,
        out_shape=jax.ShapeDtypeStruct(e.shape, F32),
        compiler_params=_params("parallel"),
        name="ssm_scan",
    )(e, a_chunk.reshape(kt, n_slabs, 1, LANES))


def _ssm_out_kernel(u_ref, x_ref, *refs, chunk, batch, seqs, group, state):
    in_refs, out_refs, (d_ref, o_ref, mq_ref) = refs[:5], refs[5:10], refs[10:]
    i = pl.program_id(1)

    @pl.when(i == 0)
    def _():
        _ssm_out_matrix(in_refs, out_refs, mq_ref, chunk=chunk, group=group, state=state)

    rows = u_ref.shape[0] // chunk
    n_chunks = rows // seqs
    n_slabs = x_ref.shape[1]
    ns2 = n_slabs * LANES
    us = _chunk_rows(u_ref, rows, chunk)
    x = jnp.concatenate(
        [jnp.concatenate([x_ref[0, sl, _batch_rows(i * seqs + s, n_chunks, batch), :] for sl in range(n_slabs)],
                         axis=1) for s in range(seqs)], axis=0)
    xu = jnp.concatenate([x] + us, axis=1).astype(BF16)
    d = d_ref[0]
    pair = 2 * LANES
    for c in range(0, chunk * LANES, pair):
        k_rows = ns2 + c + pair
        y = jnp.dot(xu[:, :k_rows], mq_ref[:k_rows, c:c + pair], preferred_element_type=F32)
        for j in (c // LANES, c // LANES + 1):
            yj = y[:, j * LANES - c:(j + 1) * LANES - c] + d * us[j]
            o_ref[pl.ds(j, rows, stride=chunk), :] = jax.nn.gelu(yj)


def _ssm_out(proj, x_prev, params, d_skip, *, u_col0, batch, seqs):
    in_arrays, out_arrays, in_specs, out_specs, dims = params
    t = proj.shape[0]
    kt, ns, chunk = dims["kt"], dims["ns"], dims["chunk"]
    _, n_slabs, n_rows, _ = x_prev.shape
    cl = chunk * LANES
    assert batch % seqs == 0 and chunk % 2 == 0 and n_slabs * LANES == 2 * ns
    tokens = t // batch * seqs
    tile = lambda k, i: k
    return pl.pallas_call(
        functools.partial(_ssm_out_kernel, chunk=chunk, batch=batch, seqs=seqs,
                          group=dims["group"], state=dims["state"]),
        grid=(kt, batch // seqs),
        in_specs=[
            pl.BlockSpec((tokens, LANES), lambda k, i: (i, u_col0 + k)),
            pl.BlockSpec((1, n_slabs, n_rows, LANES), lambda k, i: (k, 0, 0, 0)),
        ] + in_specs(tile) + out_specs(tile) + [pl.BlockSpec((1, 1, LANES), lambda k, i: (k, 0, 0))],
        out_specs=pl.BlockSpec((tokens, LANES), lambda k, i: (i, k)),
        out_shape=jax.ShapeDtypeStruct((t, kt * LANES), F32),
        scratch_shapes=[pltpu.VMEM((2 * ns + cl, cl), BF16)],
        compiler_params=_params("parallel", "arbitrary"),
        name="ssm_out",
    )(proj, x_prev, *in_arrays, *out_arrays, d_skip.astype(F32).reshape(kt, 1, LANES))


def _outproj_kernel(h_ref, a_ref, y_ref, gw_ref, gb_ref, wa_ref, wy_ref, o_ref):
    y = y_ref[...]
    z = jnp.dot(y.astype(BF16), gw_ref[...], preferred_element_type=F32) + gb_ref[...]
    yg = (y * jax.nn.sigmoid(z)).astype(BF16)
    mixed = jnp.dot(a_ref[...], wa_ref[...], preferred_element_type=F32)
    mixed = mixed + jnp.dot(yg, wy_ref[...], preferred_element_type=F32)
    o_ref[...] = h_ref[...] + mixed


def _outproj(h, attn, y, glu_w, glu_b, w_out, *, tm=512):
    t, d = h.shape
    wa = attn.shape[1]
    wy = y.shape[1]
    tm = _tile(t, tm)
    const = lambda shape: pl.BlockSpec(shape, lambda i: (0, 0))
    return pl.pallas_call(
        _outproj_kernel,
        grid=(t // tm,),
        in_specs=[
            pl.BlockSpec((tm, d), lambda i: (i, 0)),
            pl.BlockSpec((tm, wa), lambda i: (i, 0)),
            pl.BlockSpec((tm, wy), lambda i: (i, 0)),
            const((wy, wy)), const((1, wy)),
            pl.BlockSpec((wa, d), lambda i: (0, 0)),
            pl.BlockSpec((wy, d), lambda i: (wa // wy, 0)),
        ],
        out_specs=pl.BlockSpec((tm, d), lambda i: (i, 0)),
        out_shape=jax.ShapeDtypeStruct((t, d), F32),
        compiler_params=_params("parallel"),
        name="outproj",
    )(h, attn, y, glu_w, glu_b.astype(F32).reshape(1, wy), w_out, w_out)


def kernel(x, ffn1_norm, ffn1_w_gate, ffn1_w_up, ffn1_w_down, mix_norm, w_in, q_norm, k_norm, rel_bias,
           ssm_lambda_re, ssm_lambda_im, ssm_log_dt, ssm_b_re, ssm_b_im, ssm_c_re, ssm_c_im, ssm_d,
           glu_w, glu_b, w_out, ffn2_norm, ffn2_w_gate, ffn2_w_up, ffn2_w_down):
    batch, seq, d_model = x.shape
    depth = w_in.shape[0]
    head_dim = q_norm.shape[-1]
    n_heads = rel_bias.shape[1]
    attn_width = n_heads * head_dim
    ssm_width = glu_w.shape[-1]
    assert w_in.shape[-1] == 3 * attn_width + ssm_width and attn_width % ssm_width == 0
    assert ssm_width % LANES == 0 and seq % SSM_CHUNK == 0
    u_col0 = 3 * attn_width // LANES

    bias = _bias_tables(rel_bias)
    h = x.reshape(batch * seq, d_model).astype(F32)
    for l in range(depth):
        h = _ffn(h, ffn1_norm[l], ffn1_w_gate[l], ffn1_w_up[l], ffn1_w_down[l])
        proj = _proj(h, mix_norm[l], w_in[l].astype(BF16), q_norm[l], k_norm[l], attn_width=attn_width)
        attn = _attention(proj, bias, batch=batch, seq=seq, n_heads=n_heads, head_dim=head_dim)
        ssm = _ssm_params(ssm_lambda_re[l], ssm_lambda_im[l], ssm_log_dt[l],
                          ssm_b_re[l], ssm_b_im[l], ssm_c_re[l], ssm_c_im[l])
        e, a_chunk = _ssm_state(proj, ssm[0], ssm[2], ssm[4], u_col0=u_col0, batch=batch)
        x_prev = _ssm_scan(e, a_chunk, batch=batch)
        y = _ssm_out(proj, x_prev, ssm, ssm_d[l], u_col0=u_col0, batch=batch, seqs=2 if batch % 2 == 0 else 1)
        h = _outproj(h, attn, y, glu_w[l].astype(BF16), glu_b[l], w_out[l].astype(BF16))
        h = _ffn(h, ffn2_norm[l], ffn2_w_gate[l], ffn2_w_up[l], ffn2_w_down[l])
    return h.reshape(batch, seq, d_model).astype(x.dtype)
```

```python
import functools
import math

import jax
import jax.numpy as jnp
from jax import lax
from jax.experimental import pallas as pl
from jax.experimental.pallas import tpu as pltpu

F32 = jnp.float32
BF16 = jnp.bfloat16

EPS = 1e-6
NEG_INF = -1e30
DILATED_PATTERNS = ((128, 1), (512, 4), (2048, 16))
MAX_DISTANCE = 2048
SSM_CHUNK = 16
LANES = 128
VMEM_LIMIT_BYTES = 60 * 1024 * 1024


def _params(*semantics, flags=None):
    return pltpu.CompilerParams(dimension_semantics=semantics, vmem_limit_bytes=VMEM_LIMIT_BYTES, flags=flags)


def _tile(total, pref):
    if total <= pref:
        return total
    t = pref - pref % LANES
    while t > 0:
        if total % t == 0:
            return t
        t -= LANES
    return total


def _rms_norm(x, g):
    return x * lax.rsqrt(jnp.mean(x * x, axis=-1, keepdims=True) + EPS) * g


def _ffn_kernel(x_hbm, g_ref, wg_ref, wu_ref, wd_ref, o_ref, xn_ref, x_buf, x_sem):
    i, j = pl.program_id(0), pl.program_id(1)
    tm = x_buf.shape[0]

    def x_copy(tile):
        return pltpu.make_async_copy(x_hbm.at[pl.ds(tile * tm, tm), :], x_buf, x_sem)

    @pl.when((i == 0) & (j == 0))
    def _():
        x_copy(0).start()

    @pl.when(j == 0)
    def _():
        x_copy(i).wait()
        x = x_buf[...]
        xn_ref[...] = _rms_norm(x, g_ref[...]).astype(BF16)
        o_ref[...] = x

    @pl.when((j == 1) & (i + 1 < pl.num_programs(0)))
    def _():
        x_copy(i + 1).start()

    xn = xn_ref[...]
    a = jnp.dot(xn, wg_ref[...].astype(BF16), preferred_element_type=F32)
    b = jnp.dot(xn, wu_ref[...].astype(BF16), preferred_element_type=F32)
    h = (0.5 * a * jax.nn.sigmoid(a) * b).astype(BF16)
    o_ref[...] += jnp.dot(h, wd_ref[...].astype(BF16), preferred_element_type=F32)


def _ffn(x, gain, wg, wu, wd, *, tm=1024, tf=512):
    t, d = x.shape
    f = wg.shape[1]
    tm = _tile(t, tm)
    tf = _tile(f, tf)
    assert f // tf >= 2, "the x prefetch is issued at the second hidden tile"
    return pl.pallas_call(
        _ffn_kernel,
        grid=(t // tm, f // tf),
        in_specs=[
            pl.BlockSpec(memory_space=pl.ANY),
            pl.BlockSpec((1, d), lambda i, j: (0, 0)),
            pl.BlockSpec((d, tf), lambda i, j: (0, j)),
            pl.BlockSpec((d, tf), lambda i, j: (0, j)),
            pl.BlockSpec((tf, d), lambda i, j: (j, 0)),
        ],
        out_specs=pl.BlockSpec((tm, d), lambda i, j: (i, 0)),
        out_shape=jax.ShapeDtypeStruct((t, d), F32),
        scratch_shapes=[pltpu.VMEM((tm, d), BF16), pltpu.VMEM((tm, d), F32), pltpu.SemaphoreType.DMA],
        compiler_params=_params("arbitrary", "arbitrary"),
        name="ffn",
    )(x, gain.reshape(1, d), wg, wu, wd)


def _proj_kernel(h_ref, g_ref, w_ref, qkg_ref, o_ref, hn_ref, *, n_norm_tiles, head_dim):
    n = pl.program_id(1)

    @pl.when(n == 0)
    def _():
        hn_ref[...] = _rms_norm(h_ref[...], g_ref[...]).astype(BF16)

    acc = jnp.dot(hn_ref[...], w_ref[...].astype(BF16), preferred_element_type=F32)

    @pl.when(n < n_norm_tiles)
    def _():
        for c in range(0, acc.shape[1], head_dim):
            o_ref[:, c:c + head_dim] = _rms_norm(acc[:, c:c + head_dim], qkg_ref[0, :, c:c + head_dim])

    @pl.when(n >= n_norm_tiles)
    def _():
        o_ref[...] = acc


def _proj(h, gain, w_in, q_gain, k_gain, *, attn_width, tm=1024):
    t, d = h.shape
    n_out = w_in.shape[1]
    head_dim = q_gain.shape[0]
    tn = attn_width
    assert n_out % tn == 0
    n_tiles = n_out // tn
    reps = tn // head_dim
    qkg = jnp.ones((n_tiles, 1, tn), F32)
    qkg = qkg.at[0, 0].set(jnp.tile(q_gain, reps)).at[1, 0].set(jnp.tile(k_gain, reps))
    tm = _tile(t, tm)
    return pl.pallas_call(
        functools.partial(_proj_kernel, n_norm_tiles=2, head_dim=head_dim),
        grid=(t // tm, n_tiles),
        in_specs=[
            pl.BlockSpec((tm, d), lambda i, n: (i, 0)),
            pl.BlockSpec((1, d), lambda i, n: (0, 0)),
            pl.BlockSpec((d, tn), lambda i, n: (0, n)),
            pl.BlockSpec((1, 1, tn), lambda i, n: (n, 0, 0)),
        ],
        out_specs=pl.BlockSpec((tm, tn), lambda i, n: (i, n)),
        out_shape=jax.ShapeDtypeStruct((t, n_out), F32),
        scratch_shapes=[pltpu.VMEM((tm, d), BF16)],
        compiler_params=_params("parallel", "arbitrary"),
        name="proj",
    )(h, gain.reshape(1, d), w_in, qkg)


def _t5_bucket_tables(n_buckets):
    tables = []
    max_exact = n_buckets // 2
    for window, dilation in DILATED_PATTERNS:
        span = window // dilation
        qi = jnp.arange(span)[:, None]
        kj = jnp.arange(2 * span)[None, :]
        delta = qi + span - kj
        dist = jnp.maximum(delta, 0) * dilation
        d_f = jnp.maximum(dist, max_exact).astype(F32)
        large = max_exact + (jnp.log(d_f / max_exact) / math.log(MAX_DISTANCE / max_exact)
                             * (n_buckets - max_exact)).astype(jnp.int32)
        large = jnp.minimum(large, n_buckets - 1)
        bucket = jnp.where(dist < max_exact, dist, large)
        valid = (delta >= 0) & (delta <= span)
        tables.append(jnp.where(valid, bucket, -1).astype(jnp.int32))
    return jnp.stack(tables)


def _bias_kernel(rb_ref, bucket_ref, o_ref, *, n_buckets):
    h = pl.program_id(0)
    for p in range(bucket_ref.shape[0]):
        bk = bucket_ref[p]
        acc = jnp.full(bk.shape, NEG_INF, F32)
        for b in range(n_buckets):
            acc = jnp.where(bk == b, rb_ref[b, h] * math.log2(math.e), acc)
        o_ref[0, 2 * p] = acc
        in_prev_half = lax.broadcasted_iota(jnp.int32, bk.shape, 1) < bk.shape[1] // 2
        o_ref[0, 2 * p + 1] = jnp.where(in_prev_half, NEG_INF, acc)


def _bias_tables(rel_bias):
    n_buckets, n_heads = rel_bias.shape
    buckets = _t5_bucket_tables(n_buckets)
    n_pat, span, span2 = buckets.shape
    return pl.pallas_call(
        functools.partial(_bias_kernel, n_buckets=n_buckets),
        grid=(n_heads,),
        in_specs=[
            pl.BlockSpec(memory_space=pltpu.SMEM),
            pl.BlockSpec((n_pat, span, span2), lambda h: (0, 0, 0)),
        ],
        out_specs=pl.BlockSpec((1, 2 * n_pat, span, span2), lambda h: (h, 0, 0, 0)),
        out_shape=jax.ShapeDtypeStruct((n_heads, 2 * n_pat, span, span2), F32),
        compiler_params=_params("arbitrary"),
        name="bias",
    )(rel_bias.astype(F32), buckets)


ATTN_UNROLL = ((32, 16), (32, 4), (32, 4))


def _attn_kernel(q_ref, k_ref, v_ref, bias_ref, o_ref,
                 acc_a, m_a, l_a, acc_b, m_b, l_b, q4_ref, k4_ref, v4_ref, *, seq, span):
    hd = q_ref.shape[1]
    scale2 = math.log2(math.e) / math.sqrt(hd)
    nt_dims = (((1,), (1,)), ((), ()))
    g4, g16 = seq // 4, seq // 16
    state_a, state_b = (acc_a, m_a, l_a), (acc_b, m_b, l_b)
    src_nat, src_4 = (q_ref, k_ref, v_ref), (q4_ref, k4_ref, v4_ref)

    def run(n_classes, per_class, shape, src, stride, pat, state_in, state_out, save4, start_of, out_of):
        unroll, skew = shape
        assert per_class % unroll == 0 or unroll % per_class == 0
        total = n_classes * per_class
        assert total % unroll == 0

        def ld(ref, first_row):
            if stride == 1:
                return ref[pl.ds(first_row, span), :]
            return ref[pl.ds(first_row, span, stride=stride), :]

        def body(t, carry):
            blocks = []
            for u in range(unroll):
                if unroll % per_class == 0:
                    c, n = t * (unroll // per_class) + u // per_class, u % per_class
                    prev = "reuse" if n > 0 else None
                else:
                    trips_per_class = per_class // unroll
                    c, n = t // trips_per_class, (t % trips_per_class) * unroll + u
                    prev = "reuse" if u > 0 else "load"
                blocks.append(dict(n=n, prev=prev, start=start_of(c, n), rows_out=pl.ds(out_of(c, n), span)))

            def front(first):
                group = blocks[first:first + skew]
                for u, blk in enumerate(group, first):
                    q32, k32, v32 = (ld(r, blk["start"]) for r in src)
                    if save4:
                        q4_ref[blk["rows_out"], :] = q32
                        k4_ref[blk["rows_out"], :] = k32
                        v4_ref[blk["rows_out"], :] = v32
                    blk["q"], blk["k"], blk["v"] = q32.astype(BF16), k32.astype(BF16), v32.astype(BF16)
                    if blk["prev"] is None:
                        blk["kk"], blk["vv"] = blk["k"], blk["v"]
                        blk["bias"] = (2 * pat, True)
                        continue
                    if blk["prev"] == "load":
                        has_prev = jnp.minimum(blk["n"], 1)
                        first_row = blk["start"] - span * stride * has_prev
                        k_prev, v_prev = ld(src[1], first_row).astype(BF16), ld(src[2], first_row).astype(BF16)
                        blk["bias"] = (2 * pat + 1 - has_prev, False)
                    else:
                        k_prev, v_prev = blocks[u - 1]["k"], blocks[u - 1]["v"]
                        blk["bias"] = (2 * pat, False)
                    blk["kk"] = jnp.concatenate([k_prev, blk["k"]], axis=0)
                    blk["vv"] = jnp.concatenate([v_prev, blk["v"]], axis=0)
                for blk in group:
                    blk["s"] = lax.dot_general(blk["q"], blk["kk"], nt_dims, preferred_element_type=F32)

            def back(first):
                group = blocks[first:first + skew]
                for blk in group:
                    table, cur_half_only = blk["bias"]
                    bias = bias_ref[0, table]
                    s = blk["s"] * scale2 + (bias[:, span:] if cur_half_only else bias)
                    m_blk = jnp.max(s, axis=-1, keepdims=True)
                    if state_in is None:
                        m_new = jnp.broadcast_to(m_blk, (span, hd))
                    else:
                        blk["m_old"] = ld(state_in[1], blk["start"])
                        m_new = jnp.maximum(blk["m_old"], m_blk)
                    m_wide = m_new if s.shape[1] == hd else jnp.concatenate([m_new, m_new], axis=1)
                    blk["p"] = jnp.exp2(s - m_wide).astype(BF16)
                    blk["m_new"] = m_new
                for blk in group:
                    v_aug = jnp.concatenate([blk["vv"], jnp.ones_like(blk["vv"])], axis=1)
                    blk["pv"] = jnp.dot(blk["p"], v_aug, preferred_element_type=F32)
                for blk in group:
                    o_new, l_new = blk["pv"][:, :hd], blk["pv"][:, hd:]
                    if state_in is not None:
                        alpha = jnp.exp2(blk["m_old"] - blk["m_new"])
                        o_new = alpha * ld(state_in[0], blk["start"]) + o_new
                        l_new = alpha * ld(state_in[2], blk["start"]) + l_new
                    state_out[0][blk["rows_out"], :] = o_new
                    state_out[1][blk["rows_out"], :] = blk["m_new"]
                    state_out[2][blk["rows_out"], :] = l_new

            front(0)
            for first in range(0, unroll, skew):
                if first + skew < unroll:
                    front(first + skew)
                back(first)
            return carry

        lax.fori_loop(0, total // unroll, body, 0)

    blk4 = 4 * span

    run(1, seq // span, ATTN_UNROLL[0], src_nat, 1, 0, None, state_a, False,
        lambda c, n: n * span, lambda c, n: n * span)

    run(4, g4 // span, ATTN_UNROLL[1], src_nat, 4, 1, state_a, state_b, True,
        lambda c, n: n * blk4 + c, lambda c, n: c * g4 + n * span)

    run(16, g16 // span, ATTN_UNROLL[2], src_4, 4, 2, state_b, state_a, False,
        lambda c, n: (c // 4) * g4 + n * blk4 + c % 4,
        lambda c, n: (c // 4) * g4 + (c % 4) * g16 + n * span)

    for c in range(16):
        rows = pl.ds((c // 4) * g4 + (c % 4) * g16, g16)
        acc_b[pl.ds((c // 4) * g4 + c % 4, g16, stride=4), :] = acc_a[rows, :] / l_a[rows, :]
    for b in range(4):
        acc_a[pl.ds(b, g4, stride=4), :] = acc_b[pl.ds(b * g4, g4), :]
    o_ref[...] = acc_a[...].astype(o_ref.dtype)


def _attention(proj, bias, *, batch, seq, n_heads, head_dim):
    assert DILATED_PATTERNS == ((128, 1), (512, 4), (2048, 16)), "row orders are built for dilations 1, 4, 16"
    span = 128
    assert seq % DILATED_PATTERNS[-1][0] == 0
    n_tables = bias.shape[1]
    qkv_spec = lambda off: pl.BlockSpec((seq, head_dim), lambda b, h: (b, off + h))
    return pl.pallas_call(
        functools.partial(_attn_kernel, seq=seq, span=span),
        grid=(batch, n_heads),
        in_specs=[
            qkv_spec(0), qkv_spec(n_heads), qkv_spec(2 * n_heads),
            pl.BlockSpec((1, n_tables, span, 2 * span), lambda b, h: (h, 0, 0, 0)),
        ],
        out_specs=pl.BlockSpec((seq, head_dim), lambda b, h: (b, h)),
        out_shape=jax.ShapeDtypeStruct((batch * seq, n_heads * head_dim), BF16),
        scratch_shapes=[pltpu.VMEM((seq, head_dim), F32)] * 9,
        compiler_params=_params("parallel", "arbitrary"),
        name="attn",
    )(proj, proj, proj, bias)


def _ssm_discretize(lr, li, ldt):
    dt = jnp.exp(ldt)
    mag = jnp.exp(lr * dt)
    return mag * jnp.cos(li * dt), mag * jnp.sin(li * dt)


def _ssm_in_blocks(in_refs, n_powers, store, *, group, state):
    lr_ref, li_ref, ldt_ref, bre_ref, bim_ref = in_refs
    gpt = LANES // group
    ns = gpt * state
    lr, li = lr_ref[0], li_ref[0]
    ar, ai = _ssm_discretize(lr, li, ldt_ref[0])
    zr, zi = ar - 1.0, ai
    lam_sq = lr * lr + li * li
    coef_re = (zr * lr + zi * li) / lam_sq
    coef_im = (zi * lr - zr * li) / lam_sq
    bre, bim = bre_ref[0], bim_ref[0]
    bbar_re = coef_re * bre - coef_im * bim
    bbar_im = coef_re * bim + coef_im * bre

    row_g = lax.broadcasted_iota(jnp.int32, (LANES, ns), 0) // group
    col_g = lax.broadcasted_iota(jnp.int32, (LANES, ns), 1) // state
    mask_in = row_g == col_g

    def expand_in(w):
        return jnp.where(mask_in, jnp.concatenate([w] * gpt, axis=0), 0.0)

    pr = jnp.ones_like(ar)
    pi = jnp.zeros_like(ai)
    for m in range(n_powers):
        w_re = pr * bbar_re - pi * bbar_im
        w_im = pr * bbar_im + pi * bbar_re
        store(m, jnp.concatenate([expand_in(w_re), expand_in(w_im)], axis=1).astype(BF16))
        pr, pi = pr * ar - pi * ai, pr * ai + pi * ar
    return pr, pi


def _ssm_out_matrix(in_refs, out_refs, mq_ref, *, chunk, group, state):
    lr_ref, li_ref, ldt_ref, cre_ref, cim_ref = out_refs
    gpt = LANES // group
    ns = gpt * state
    first = []
    _ssm_in_blocks(in_refs, 1, lambda m, blk: first.append(blk), group=group, state=state)
    pe0 = first[0]

    acr, aci = _ssm_discretize(lr_ref[0], li_ref[0], ldt_ref[0])
    cre, cim = cre_ref[0], cim_ref[0]
    row_g = lax.broadcasted_iota(jnp.int32, (ns, LANES), 0) // state
    col_g = lax.broadcasted_iota(jnp.int32, (ns, LANES), 1) // group
    mask_out = row_g == col_g

    def expand_out(z):
        return jnp.where(mask_out, jnp.concatenate([z] * gpt, axis=0), 0.0)

    qr = jnp.ones_like(acr)
    qi = jnp.zeros_like(aci)
    zeros = jnp.zeros((LANES, LANES), BF16)
    for m in range(chunk + 1):
        z_re = expand_out(cre * qr - cim * qi)
        z_im = expand_out(-(cre * qi + cim * qr))
        qc_m = jnp.concatenate([z_re, z_im], axis=0).astype(BF16)
        if m >= 1:
            mq_ref[:2 * ns, (m - 1) * LANES:m * LANES] = qc_m
        if m < chunk:
            bd = jnp.dot(pe0, qc_m, preferred_element_type=F32).astype(BF16)
            for j in range(chunk - m):
                mq_ref[2 * ns + j * LANES:2 * ns + (j + 1) * LANES, (j + m) * LANES:(j + m + 1) * LANES] = bd
            if m >= 1:
                for j in range(m, chunk):
                    mq_ref[2 * ns + j * LANES:2 * ns + (j + 1) * LANES, (j - m) * LANES:(j - m + 1) * LANES] = zeros
        qr, qi = qr * acr - qi * aci, qr * aci + qi * acr


def _ssm_params(lam_re, lam_im, log_dt, b_re, b_im, c_re, c_im):
    n_groups, state, group = b_re.shape
    gpt = LANES // group
    kt = n_groups // gpt
    ns = gpt * state

    def row(a):
        return a.astype(F32).reshape(kt, 1, ns)

    def col(a):
        return jnp.repeat(jnp.transpose(a.astype(F32).reshape(kt, gpt, state), (0, 2, 1)), group, axis=-1)

    ldt = jnp.broadcast_to(log_dt.astype(F32)[:, None], (n_groups, state))
    b_cp = lambda b: jnp.transpose(b.astype(F32), (2, 0, 1)).reshape(group, kt, ns).transpose(1, 0, 2)
    c_pc = lambda c: jnp.transpose(c.astype(F32).reshape(kt, gpt, group, state), (0, 3, 1, 2)).reshape(kt, state, LANES)

    in_arrays = (row(lam_re), row(lam_im), row(ldt), b_cp(b_re), b_cp(b_im))
    out_arrays = (col(lam_re), col(lam_im), col(ldt), c_pc(c_re), c_pc(c_im))

    def in_specs(tile_of):
        row_spec = pl.BlockSpec((1, 1, ns), lambda *g: (tile_of(*g), 0, 0))
        b_spec = pl.BlockSpec((1, group, ns), lambda *g: (tile_of(*g), 0, 0))
        return [row_spec, row_spec, row_spec, b_spec, b_spec]

    def out_specs(tile_of):
        return [pl.BlockSpec((1, state, LANES), lambda *g: (tile_of(*g), 0, 0))] * 5

    dims = dict(kt=kt, ns=ns, group=group, state=state, chunk=SSM_CHUNK)
    return in_arrays, out_arrays, in_specs, out_specs, dims


def _chunk_rows(u_ref, rows, chunk):
    return [u_ref[pl.ds(j, rows, stride=chunk), :] for j in range(chunk)]


def _batch_rows(first, count, batch):
    return pl.ds(first, count) if batch == 1 else pl.ds(first, count, stride=batch)


def _ssm_state_kernel(u_ref, lr_ref, li_ref, ldt_ref, bre_ref, bim_ref, e_ref, a_ref, pe_ref, *,
                      chunk, batch, group, state):
    n_slabs, n_rows = e_ref.shape[1], e_ref.shape[2]
    n_chunks = n_rows // batch

    def store_pe(m, blk):
        j = chunk - 1 - m
        pe_ref[j * LANES:(j + 1) * LANES, :] = blk

    pr, pi = _ssm_in_blocks((lr_ref, li_ref, ldt_ref, bre_ref, bim_ref), chunk, store_pe,
                            group=group, state=state)
    a_ref[0] = jnp.concatenate([pr, pi], axis=1)

    u = jnp.concatenate(_chunk_rows(u_ref, n_rows, chunk), axis=1).astype(BF16)
    e = jnp.dot(u, pe_ref[...], preferred_element_type=F32)
    for b in range(batch):
        for s in range(n_slabs):
            e_ref[0, s, _batch_rows(b, n_chunks, batch), :] = (
                e[b * n_chunks:(b + 1) * n_chunks, s * LANES:(s + 1) * LANES])


def _ssm_state(proj, in_arrays, in_specs, dims, *, u_col0, batch):
    t = proj.shape[0]
    kt, ns, chunk = dims["kt"], dims["ns"], dims["chunk"]
    n_rows = t // chunk
    n_slabs = 2 * ns // LANES
    return pl.pallas_call(
        functools.partial(_ssm_state_kernel, chunk=chunk, batch=batch, group=dims["group"], state=dims["state"]),
        grid=(kt,),
        in_specs=[pl.BlockSpec((t, LANES), lambda k: (0, u_col0 + k))] + in_specs(lambda k: k),
        out_specs=[
            pl.BlockSpec((1, n_slabs, n_rows, LANES), lambda k: (k, 0, 0, 0)),
            pl.BlockSpec((1, 1, 2 * ns), lambda k: (k, 0, 0)),
        ],
        out_shape=[
            jax.ShapeDtypeStruct((kt, n_slabs, n_rows, LANES), F32),
            jax.ShapeDtypeStruct((kt, 1, 2 * ns), F32),
        ],
        scratch_shapes=[pltpu.VMEM((chunk * LANES, 2 * ns), BF16)],
        compiler_params=_params("parallel"),
        name="ssm_state",
    )(proj, *in_arrays)


def _ssm_scan_kernel(e_ref, a_ref, x_ref, *, batch):
    n_slabs, n_rows = e_ref.shape[1], e_ref.shape[2]
    half = n_slabs // 2
    n_chunks = n_rows // batch
    ar = jnp.broadcast_to(a_ref[0, :half], (half, batch, LANES))
    ai = jnp.broadcast_to(a_ref[0, half:], (half, batch, LANES))

    def step(n, carry):
        xr, xi = carry
        rows = pl.ds(n * batch, batch)
        x_ref[0, :half, rows, :] = xr
        x_ref[0, half:, rows, :] = xi
        er = e_ref[0, :half, rows, :]
        ei = e_ref[0, half:, rows, :]
        return ar * xr - ai * xi + er, ar * xi + ai * xr + ei

    zero = jnp.zeros((half, batch, LANES), F32)
    lax.fori_loop(0, n_chunks, step, (zero, zero))


def _ssm_scan(e, a_chunk, *, batch):
    kt, n_slabs, n_rows, _ = e.shape
    spec = pl.BlockSpec((1, n_slabs, n_rows, LANES), lambda k: (k, 0, 0, 0))
    return pl.pallas_call(
        functools.partial(_ssm_scan_kernel, batch=batch),
        grid=(kt,),
        in_specs=[spec, pl.BlockSpec((1, n_slabs, 1, LANES), lambda k: (k, 0, 0, 0))],
        out_specs=spec,
        out_shape=jax.ShapeDtypeStruct(e.shape, F32),
        compiler_params=_params("parallel"),
        name="ssm_scan",
    )(e, a_chunk.reshape(kt, n_slabs, 1, LANES))


def _ssm_out_kernel(u_ref, x_ref, *refs, chunk, batch, seqs, group, state):
    in_refs, out_refs, (d_ref, o_ref, mq_ref) = refs[:5], refs[5:10], refs[10:]
    i = pl.program_id(1)

    @pl.when(i == 0)
    def _():
        _ssm_out_matrix(in_refs, out_refs, mq_ref, chunk=chunk, group=group, state=state)

    rows = u_ref.shape[0] // chunk
    n_chunks = rows // seqs
    n_slabs = x_ref.shape[1]
    ns2 = n_slabs * LANES
    us = _chunk_rows(u_ref, rows, chunk)
    x = jnp.concatenate(
        [jnp.concatenate([x_ref[0, sl, _batch_rows(i * seqs + s, n_chunks, batch), :] for sl in range(n_slabs)],
                         axis=1) for s in range(seqs)], axis=0)
    xu = jnp.concatenate([x] + us, axis=1).astype(BF16)
    d = d_ref[0]
    pair = 2 * LANES
    for c in range(0, chunk * LANES, pair):
        k_rows = ns2 + c + pair
        y = jnp.dot(xu[:, :k_rows], mq_ref[:k_rows, c:c + pair], preferred_element_type=F32)
        for j in (c // LANES, c // LANES + 1):
            yj = y[:, j * LANES - c:(j + 1) * LANES - c] + d * us[j]
            o_ref[pl.ds(j, rows, stride=chunk), :] = jax.nn.gelu(yj)


def _ssm_out(proj, x_prev, params, d_skip, *, u_col0, batch, seqs):
    in_arrays, out_arrays, in_specs, out_specs, dims = params
    t = proj.shape[0]
    kt, ns, chunk = dims["kt"], dims["ns"], dims["chunk"]
    _, n_slabs, n_rows, _ = x_prev.shape
    cl = chunk * LANES
    assert batch % seqs == 0 and chunk % 2 == 0 and n_slabs * LANES == 2 * ns
    tokens = t // batch * seqs
    tile = lambda k, i: k
    return pl.pallas_call(
        functools.partial(_ssm_out_kernel, chunk=chunk, batch=batch, seqs=seqs,
                          group=dims["group"], state=dims["state"]),
        grid=(kt, batch // seqs),
        in_specs=[
            pl.BlockSpec((tokens, LANES), lambda k, i: (i, u_col0 + k)),
            pl.BlockSpec((1, n_slabs, n_rows, LANES), lambda k, i: (k, 0, 0, 0)),
        ] + in_specs(tile) + out_specs(tile) + [pl.BlockSpec((1, 1, LANES), lambda k, i: (k, 0, 0))],
        out_specs=pl.BlockSpec((tokens, LANES), lambda k, i: (i, k)),
        out_shape=jax.ShapeDtypeStruct((t, kt * LANES), F32),
        scratch_shapes=[pltpu.VMEM((2 * ns + cl, cl), BF16)],
        compiler_params=_params("parallel", "arbitrary"),
        name="ssm_out",
    )(proj, x_prev, *in_arrays, *out_arrays, d_skip.astype(F32).reshape(kt, 1, LANES))


def _outproj_kernel(h_ref, a_ref, y_ref, gw_ref, gb_ref, wa_ref, wy_ref, o_ref):
    y = y_ref[...]
    z = jnp.dot(y.astype(BF16), gw_ref[...], preferred_element_type=F32) + gb_ref[...]
    yg = (y * jax.nn.sigmoid(z)).astype(BF16)
    mixed = jnp.dot(a_ref[...], wa_ref[...], preferred_element_type=F32)
    mixed = mixed + jnp.dot(yg, wy_ref[...], preferred_element_type=F32)
    o_ref[...] = h_ref[...] + mixed


def _outproj(h, attn, y, glu_w, glu_b, w_out, *, tm=512):
    t, d = h.shape
    wa = attn.shape[1]
    wy = y.shape[1]
    tm = _tile(t, tm)
    const = lambda shape: pl.BlockSpec(shape, lambda i: (0, 0))
    return pl.pallas_call(
        _outproj_kernel,
        grid=(t // tm,),
        in_specs=[
            pl.BlockSpec((tm, d), lambda i: (i, 0)),
            pl.BlockSpec((tm, wa), lambda i: (i, 0)),
            pl.BlockSpec((tm, wy), lambda i: (i, 0)),
            const((wy, wy)), const((1, wy)),
            pl.BlockSpec((wa, d), lambda i: (0, 0)),
            pl.BlockSpec((wy, d), lambda i: (wa // wy, 0)),
        ],
        out_specs=pl.BlockSpec((tm, d), lambda i: (i, 0)),
        out_shape=jax.ShapeDtypeStruct((t, d), F32),
        compiler_params=_params("parallel"),
        name="outproj",
    )(h, attn, y, glu_w, glu_b.astype(F32).reshape(1, wy), w_out, w_out)


def kernel(x, ffn1_norm, ffn1_w_gate, ffn1_w_up, ffn1_w_down, mix_norm, w_in, q_norm, k_norm, rel_bias,
           ssm_lambda_re, ssm_lambda_im, ssm_log_dt, ssm_b_re, ssm_b_im, ssm_c_re, ssm_c_im, ssm_d,
           glu_w, glu_b, w_out, ffn2_norm, ffn2_w_gate, ffn2_w_up, ffn2_w_down):
    batch, seq, d_model = x.shape
    depth = w_in.shape[0]
    head_dim = q_norm.shape[-1]
    n_heads = rel_bias.shape[1]
    attn_width = n_heads * head_dim
    ssm_width = glu_w.shape[-1]
    assert w_in.shape[-1] == 3 * attn_width + ssm_width and attn_width % ssm_width == 0
    assert ssm_width % LANES == 0 and seq % SSM_CHUNK == 0
    u_col0 = 3 * attn_width // LANES

    bias = _bias_tables(rel_bias)
    h = x.reshape(batch * seq, d_model).astype(F32)
    for l in range(depth):
        h = _ffn(h, ffn1_norm[l], ffn1_w_gate[l], ffn1_w_up[l], ffn1_w_down[l])
        proj = _proj(h, mix_norm[l], w_in[l].astype(BF16), q_norm[l], k_norm[l], attn_width=attn_width)
        attn = _attention(proj, bias, batch=batch, seq=seq, n_heads=n_heads, head_dim=head_dim)
        ssm = _ssm_params(ssm_lambda_re[l], ssm_lambda_im[l], ssm_log_dt[l],
                          ssm_b_re[l], ssm_b_im[l], ssm_c_re[l], ssm_c_im[l])
        e, a_chunk = _ssm_state(proj, ssm[0], ssm[2], ssm[4], u_col0=u_col0, batch=batch)
        x_prev = _ssm_scan(e, a_chunk, batch=batch)
        y = _ssm_out(proj, x_prev, ssm, ssm_d[l], u_col0=u_col0, batch=batch, seqs=2 if batch % 2 == 0 else 1)
        h = _outproj(h, attn, y, glu_w[l].astype(BF16), glu_b[l], w_out[l].astype(BF16))
        h = _ffn(h, ffn2_norm[l], ffn2_w_gate[l], ffn2_w_up[l], ffn2_w_down[l])
    return h.reshape(batch, seq, d_model).astype(x.dtype)
```

```python
import functools
import math

import jax
import jax.numpy as jnp
from jax import lax
from jax.experimental import pallas as pl
from jax.experimental.pallas import tpu as pltpu

F32 = jnp.float32
BF16 = jnp.bfloat16

EPS = 1e-6
NEG_INF = -1e30
DILATED_PATTERNS = ((128, 1), (512, 4), (2048, 16))
MAX_DISTANCE = 2048
SSM_CHUNK = 16
LANES = 128
VMEM_LIMIT_BYTES = 60 * 1024 * 1024


def _params(*semantics, flags=None):
    return pltpu.CompilerParams(dimension_semantics=semantics, vmem_limit_bytes=VMEM_LIMIT_BYTES, flags=flags)


def _tile(total, pref):
    if total <= pref:
        return total
    t = pref - pref % LANES
    while t > 0:
        if total % t == 0:
            return t
        t -= LANES
    return total


def _rms_norm(x, g):
    return x * lax.rsqrt(jnp.mean(x * x, axis=-1, keepdims=True) + EPS) * g


def _ffn_kernel(x_hbm, g_ref, wg_ref, wu_ref, wd_ref, o_ref, xn_ref, x_buf, x_sem):
    i, j = pl.program_id(0), pl.program_id(1)
    tm = x_buf.shape[0]

    def x_copy(tile):
        return pltpu.make_async_copy(x_hbm.at[pl.ds(tile * tm, tm), :], x_buf, x_sem)

    @pl.when((i == 0) & (j == 0))
    def _():
        x_copy(0).start()

    @pl.when(j == 0)
    def _():
        x_copy(i).wait()
        x = x_buf[...]
        xn_ref[...] = _rms_norm(x, g_ref[...]).astype(BF16)
        o_ref[...] = x

    @pl.when((j == 1) & (i + 1 < pl.num_programs(0)))
    def _():
        x_copy(i + 1).start()

    xn = xn_ref[...]
    tf = wg_ref.shape[1]
    half = tf // 2
    hs = []
    pending = None
    for c in (0, half):
        a = jnp.dot(xn, wg_ref[:, c:c + half].astype(BF16), preferred_element_type=F32)
        b = jnp.dot(xn, wu_ref[:, c:c + half].astype(BF16), preferred_element_type=F32)
        if pending is not None:
            pa, pb = pending
            hs.append((0.5 * pa * jax.nn.sigmoid(pa) * pb).astype(BF16))
        pending = (a, b)
    pa, pb = pending
    hs.append((0.5 * pa * jax.nn.sigmoid(pa) * pb).astype(BF16))
    h = jnp.concatenate(hs, axis=1)
    o_ref[...] += jnp.dot(h, wd_ref[...].astype(BF16), preferred_element_type=F32)


def _ffn(x, gain, wg, wu, wd, *, tm=1024, tf=512):
    t, d = x.shape
    f = wg.shape[1]
    tm = _tile(t, tm)
    tf = _tile(f, tf)
    assert f // tf >= 2, "the x prefetch is issued at the second hidden tile"
    return pl.pallas_call(
        _ffn_kernel,
        grid=(t // tm, f // tf),
        in_specs=[
            pl.BlockSpec(memory_space=pl.ANY),
            pl.BlockSpec((1, d), lambda i, j: (0, 0)),
            pl.BlockSpec((d, tf), lambda i, j: (0, j)),
            pl.BlockSpec((d, tf), lambda i, j: (0, j)),
            pl.BlockSpec((tf, d), lambda i, j: (j, 0)),
        ],
        out_specs=pl.BlockSpec((tm, d), lambda i, j: (i, 0)),
        out_shape=jax.ShapeDtypeStruct((t, d), F32),
        scratch_shapes=[pltpu.VMEM((tm, d), BF16), pltpu.VMEM((tm, d), F32), pltpu.SemaphoreType.DMA],
        compiler_params=_params("arbitrary", "arbitrary"),
        name="ffn",
    )(x, gain.reshape(1, d), wg, wu, wd)


def _proj_kernel(h_ref, g_ref, w_ref, qkg_ref, o_ref, hn_ref, *, n_norm_tiles, head_dim):
    n = pl.program_id(1)

    @pl.when(n == 0)
    def _():
        hn_ref[...] = _rms_norm(h_ref[...], g_ref[...]).astype(BF16)

    hn = hn_ref[...]
    normed_tile = n < n_norm_tiles
    chunk = 2 * head_dim

    def finish(c, acc):
        for h0 in range(0, chunk, head_dim):
            x = acc[:, h0:h0 + head_dim]
            cols = slice(c + h0, c + h0 + head_dim)
            o_ref[:, cols] = jnp.where(normed_tile, _rms_norm(x, qkg_ref[0, :, cols]), x)

    pending = None
    for c in range(0, w_ref.shape[1], chunk):
        acc = jnp.dot(hn, w_ref[:, c:c + chunk].astype(BF16), preferred_element_type=F32)
        if pending is not None:
            finish(*pending)
        pending = (c, acc)
    finish(*pending)


def _proj(h, gain, w_in, q_gain, k_gain, *, attn_width, tm=1024):
    t, d = h.shape
    n_out = w_in.shape[1]
    head_dim = q_gain.shape[0]
    tn = attn_width
    assert n_out % tn == 0
    n_tiles = n_out // tn
    reps = tn // head_dim
    qkg = jnp.ones((n_tiles, 1, tn), F32)
    qkg = qkg.at[0, 0].set(jnp.tile(q_gain, reps)).at[1, 0].set(jnp.tile(k_gain, reps))
    tm = _tile(t, tm)
    return pl.pallas_call(
        functools.partial(_proj_kernel, n_norm_tiles=2, head_dim=head_dim),
        grid=(t // tm, n_tiles),
        in_specs=[
            pl.BlockSpec((tm, d), lambda i, n: (i, 0)),
            pl.BlockSpec((1, d), lambda i, n: (0, 0)),
            pl.BlockSpec((d, tn), lambda i, n: (0, n)),
            pl.BlockSpec((1, 1, tn), lambda i, n: (n, 0, 0)),
        ],
        out_specs=pl.BlockSpec((tm, tn), lambda i, n: (i, n)),
        out_shape=jax.ShapeDtypeStruct((t, n_out), F32),
        scratch_shapes=[pltpu.VMEM((tm, d), BF16)],
        compiler_params=_params("parallel", "arbitrary"),
        name="proj",
    )(h, gain.reshape(1, d), w_in, qkg)


def _t5_bucket_tables(n_buckets):
    tables = []
    max_exact = n_buckets // 2
    for window, dilation in DILATED_PATTERNS:
        span = window // dilation
        qi = jnp.arange(span)[:, None]
        kj = jnp.arange(2 * span)[None, :]
        delta = qi + span - kj
        dist = jnp.maximum(delta, 0) * dilation
        d_f = jnp.maximum(dist, max_exact).astype(F32)
        large = max_exact + (jnp.log(d_f / max_exact) / math.log(MAX_DISTANCE / max_exact)
                             * (n_buckets - max_exact)).astype(jnp.int32)
        large = jnp.minimum(large, n_buckets - 1)
        bucket = jnp.where(dist < max_exact, dist, large)
        valid = (delta >= 0) & (delta <= span)
        tables.append(jnp.where(valid, bucket, -1).astype(jnp.int32))
    return jnp.stack(tables)


def _bias_kernel(rb_ref, bucket_ref, o_ref, *, n_buckets):
    h = pl.program_id(0)
    for p in range(bucket_ref.shape[0]):
        bk = bucket_ref[p]
        acc = jnp.full(bk.shape, NEG_INF, F32)
        for b in range(n_buckets):
            acc = jnp.where(bk == b, rb_ref[b, h] * math.log2(math.e), acc)
        o_ref[0, 2 * p] = acc
        in_prev_half = lax.broadcasted_iota(jnp.int32, bk.shape, 1) < bk.shape[1] // 2
        o_ref[0, 2 * p + 1] = jnp.where(in_prev_half, NEG_INF, acc)


def _bias_tables(rel_bias):
    n_buckets, n_heads = rel_bias.shape
    buckets = _t5_bucket_tables(n_buckets)
    n_pat, span, span2 = buckets.shape
    return pl.pallas_call(
        functools.partial(_bias_kernel, n_buckets=n_buckets),
        grid=(n_heads,),
        in_specs=[
            pl.BlockSpec(memory_space=pltpu.SMEM),
            pl.BlockSpec((n_pat, span, span2), lambda h: (0, 0, 0)),
        ],
        out_specs=pl.BlockSpec((1, 2 * n_pat, span, span2), lambda h: (h, 0, 0, 0)),
        out_shape=jax.ShapeDtypeStruct((n_heads, 2 * n_pat, span, span2), F32),
        compiler_params=_params("arbitrary"),
        name="bias",
    )(rel_bias.astype(F32), buckets)


ATTN_UNROLL = ((32, 16), (32, 4), (32, 4))


def _attn_kernel(q_ref, k_ref, v_ref, bias_ref, o_ref,
                 acc_a, m_a, l_a, acc_b, m_b, l_b, q4_ref, k4_ref, v4_ref, *, seq, span):
    hd = q_ref.shape[1]
    scale2 = math.log2(math.e) / math.sqrt(hd)
    nt_dims = (((1,), (1,)), ((), ()))
    g4, g16 = seq // 4, seq // 16
    state_a, state_b = (acc_a, m_a, l_a), (acc_b, m_b, l_b)
    src_nat, src_4 = (q_ref, k_ref, v_ref), (q4_ref, k4_ref, v4_ref)

    def run(n_classes, per_class, shape, src, stride, pat, state_in, state_out, save4, start_of, out_of):
        unroll, skew = shape
        assert per_class % unroll == 0 or unroll % per_class == 0
        total = n_classes * per_class
        assert total % unroll == 0

        def ld(ref, first_row):
            if stride == 1:
                return ref[pl.ds(first_row, span), :]
            return ref[pl.ds(first_row, span, stride=stride), :]

        def body(t, carry):
            blocks = []
            for u in range(unroll):
                if unroll % per_class == 0:
                    c, n = t * (unroll // per_class) + u // per_class, u % per_class
                    prev = "reuse" if n > 0 else None
                else:
                    trips_per_class = per_class // unroll
                    c, n = t // trips_per_class, (t % trips_per_class) * unroll + u
                    prev = "reuse" if u > 0 else "load"
                blocks.append(dict(n=n, prev=prev, start=start_of(c, n), rows_out=pl.ds(out_of(c, n), span)))

            def front(first):
                group = blocks[first:first + skew]
                for u, blk in enumerate(group, first):
                    q32, k32, v32 = (ld(r, blk["start"]) for r in src)
                    if save4:
                        q4_ref[blk["rows_out"], :] = q32
                        k4_ref[blk["rows_out"], :] = k32
                        v4_ref[blk["rows_out"], :] = v32
                    blk["q"], blk["k"], blk["v"] = q32.astype(BF16), k32.astype(BF16), v32.astype(BF16)
                    if blk["prev"] is None:
                        blk["kk"], blk["vv"] = blk["k"], blk["v"]
                        blk["bias"] = (2 * pat, True)
                        continue
                    if blk["prev"] == "load":
                        has_prev = jnp.minimum(blk["n"], 1)
                        first_row = blk["start"] - span * stride * has_prev
                        k_prev, v_prev = ld(src[1], first_row).astype(BF16), ld(src[2], first_row).astype(BF16)
                        blk["bias"] = (2 * pat + 1 - has_prev, False)
                    else:
                        k_prev, v_prev = blocks[u - 1]["k"], blocks[u - 1]["v"]
                        blk["bias"] = (2 * pat, False)
                    blk["kk"] = jnp.concatenate([k_prev, blk["k"]], axis=0)
                    blk["vv"] = jnp.concatenate([v_prev, blk["v"]], axis=0)
                for blk in group:
                    blk["s"] = lax.dot_general(blk["q"], blk["kk"], nt_dims, preferred_element_type=F32)

            def middle(first):
                group = blocks[first:first + skew]
                for blk in group:
                    table, cur_half_only = blk["bias"]
                    bias = bias_ref[0, table]
                    s = blk["s"] * scale2 + (bias[:, span:] if cur_half_only else bias)
                    m_blk = jnp.max(s, axis=-1, keepdims=True)
                    if state_in is None:
                        m_new = jnp.broadcast_to(m_blk, (span, hd))
                    else:
                        blk["m_old"] = ld(state_in[1], blk["start"])
                        m_new = jnp.maximum(blk["m_old"], m_blk)
                    m_wide = m_new if s.shape[1] == hd else jnp.concatenate([m_new, m_new], axis=1)
                    blk["p"] = jnp.exp2(s - m_wide).astype(BF16)
                    blk["m_new"] = m_new

            def back(first):
                group = blocks[first:first + skew]
                for blk in group:
                    v_aug = jnp.concatenate([blk["vv"], jnp.ones_like(blk["vv"])], axis=1)
                    blk["pv"] = jnp.dot(blk["p"], v_aug, preferred_element_type=F32)
                for blk in group:
                    o_new, l_new = blk["pv"][:, :hd], blk["pv"][:, hd:]
                    if state_in is not None:
                        alpha = jnp.exp2(blk["m_old"] - blk["m_new"])
                        o_new = alpha * ld(state_in[0], blk["start"]) + o_new
                        l_new = alpha * ld(state_in[2], blk["start"]) + l_new
                    state_out[0][blk["rows_out"], :] = o_new
                    state_out[1][blk["rows_out"], :] = blk["m_new"]
                    state_out[2][blk["rows_out"], :] = l_new

            front(0)
            for first in range(0, unroll, skew):
                if first + skew < unroll:
                    front(first + skew)
                middle(first)
                back(first)
            return carry

        lax.fori_loop(0, total // unroll, body, 0)

    blk4 = 4 * span

    run(1, seq // span, ATTN_UNROLL[0], src_nat, 1, 0, None, state_a, False,
        lambda c, n: n * span, lambda c, n: n * span)

    run(4, g4 // span, ATTN_UNROLL[1], src_nat, 4, 1, state_a, state_b, True,
        lambda c, n: n * blk4 + c, lambda c, n: c * g4 + n * span)

    run(16, g16 // span, ATTN_UNROLL[2], src_4, 4, 2, state_b, state_a, False,
        lambda c, n: (c // 4) * g4 + n * blk4 + c % 4,
        lambda c, n: (c // 4) * g4 + (c % 4) * g16 + n * span)

    for c in range(16):
        rows = pl.ds((c // 4) * g4 + (c % 4) * g16, g16)
        acc_b[pl.ds((c // 4) * g4 + c % 4, g16, stride=4), :] = acc_a[rows, :] / l_a[rows, :]
    for b in range(4):
        acc_a[pl.ds(b, g4, stride=4), :] = acc_b[pl.ds(b * g4, g4), :]
    o_ref[...] = acc_a[...].astype(o_ref.dtype)


def _attention(proj, bias, *, batch, seq, n_heads, head_dim):
    assert DILATED_PATTERNS == ((128, 1), (512, 4), (2048, 16)), "row orders are built for dilations 1, 4, 16"
    span = 128
    assert seq % DILATED_PATTERNS[-1][0] == 0
    n_tables = bias.shape[1]
    qkv_spec = lambda off: pl.BlockSpec((seq, head_dim), lambda b, h: (b, off + h))
    return pl.pallas_call(
        functools.partial(_attn_kernel, seq=seq, span=span),
        grid=(batch, n_heads),
        in_specs=[
            qkv_spec(0), qkv_spec(n_heads), qkv_spec(2 * n_heads),
            pl.BlockSpec((1, n_tables, span, 2 * span), lambda b, h: (h, 0, 0, 0)),
        ],
        out_specs=pl.BlockSpec((seq, head_dim), lambda b, h: (b, h)),
        out_shape=jax.ShapeDtypeStruct((batch * seq, n_heads * head_dim), BF16),
        scratch_shapes=[pltpu.VMEM((seq, head_dim), F32)] * 9,
        compiler_params=_params("parallel", "arbitrary"),
        name="attn",
    )(proj, proj, proj, bias)


def _ssm_discretize(lr, li, ldt):
    dt = jnp.exp(ldt)
    mag = jnp.exp(lr * dt)
    return mag * jnp.cos(li * dt), mag * jnp.sin(li * dt)


def _ssm_in_blocks(in_refs, n_powers, store, *, group, state):
    lr_ref, li_ref, ldt_ref, bre_ref, bim_ref = in_refs
    gpt = LANES // group
    ns = gpt * state
    lr, li = lr_ref[0], li_ref[0]
    ar, ai = _ssm_discretize(lr, li, ldt_ref[0])
    zr, zi = ar - 1.0, ai
    lam_sq = lr * lr + li * li
    coef_re = (zr * lr + zi * li) / lam_sq
    coef_im = (zi * lr - zr * li) / lam_sq
    bre, bim = bre_ref[0], bim_ref[0]
    bbar_re = coef_re * bre - coef_im * bim
    bbar_im = coef_re * bim + coef_im * bre

    row_g = lax.broadcasted_iota(jnp.int32, (LANES, ns), 0) // group
    col_g = lax.broadcasted_iota(jnp.int32, (LANES, ns), 1) // state
    mask_in = row_g == col_g

    def expand_in(w):
        return jnp.where(mask_in, jnp.concatenate([w] * gpt, axis=0), 0.0)

    pr = jnp.ones_like(ar)
    pi = jnp.zeros_like(ai)
    for m in range(n_powers):
        w_re = pr * bbar_re - pi * bbar_im
        w_im = pr * bbar_im + pi * bbar_re
        store(m, jnp.concatenate([expand_in(w_re), expand_in(w_im)], axis=1).astype(BF16))
        pr, pi = pr * ar - pi * ai, pr * ai + pi * ar
    return pr, pi


def _ssm_out_matrix(in_refs, out_refs, mq_ref, *, chunk, group, state):
    lr_ref, li_ref, ldt_ref, cre_ref, cim_ref = out_refs
    gpt = LANES // group
    ns = gpt * state
    first = []
    _ssm_in_blocks(in_refs, 1, lambda m, blk: first.append(blk), group=group, state=state)
    pe0 = first[0]

    acr, aci = _ssm_discretize(lr_ref[0], li_ref[0], ldt_ref[0])
    cre, cim = cre_ref[0], cim_ref[0]
    row_g = lax.broadcasted_iota(jnp.int32, (ns, LANES), 0) // state
    col_g = lax.broadcasted_iota(jnp.int32, (ns, LANES), 1) // group
    mask_out = row_g == col_g

    def expand_out(z):
        return jnp.where(mask_out, jnp.concatenate([z] * gpt, axis=0), 0.0)

    qr = jnp.ones_like(acr)
    qi = jnp.zeros_like(aci)
    zeros = jnp.zeros((LANES, LANES), BF16)
    for m in range(chunk + 1):
        z_re = expand_out(cre * qr - cim * qi)
        z_im = expand_out(-(cre * qi + cim * qr))
        qc_m = jnp.concatenate([z_re, z_im], axis=0).astype(BF16)
        if m >= 1:
            mq_ref[:2 * ns, (m - 1) * LANES:m * LANES] = qc_m
        if m < chunk:
            bd = jnp.dot(pe0, qc_m, preferred_element_type=F32).astype(BF16)
            for j in range(chunk - m):
                mq_ref[2 * ns + j * LANES:2 * ns + (j + 1) * LANES, (j + m) * LANES:(j + m + 1) * LANES] = bd
            if m >= 1:
                for j in range(m, chunk):
                    mq_ref[2 * ns + j * LANES:2 * ns + (j + 1) * LANES, (j - m) * LANES:(j - m + 1) * LANES] = zeros
        qr, qi = qr * acr - qi * aci, qr * aci + qi * acr


def _ssm_params(lam_re, lam_im, log_dt, b_re, b_im, c_re, c_im):
    n_groups, state, group = b_re.shape
    gpt = LANES // group
    kt = n_groups // gpt
    ns = gpt * state

    def row(a):
        return a.astype(F32).reshape(kt, 1, ns)

    def col(a):
        return jnp.repeat(jnp.transpose(a.astype(F32).reshape(kt, gpt, state), (0, 2, 1)), group, axis=-1)

    ldt = jnp.broadcast_to(log_dt.astype(F32)[:, None], (n_groups, state))
    b_cp = lambda b: jnp.transpose(b.astype(F32), (2, 0, 1)).reshape(group, kt, ns).transpose(1, 0, 2)
    c_pc = lambda c: jnp.transpose(c.astype(F32).reshape(kt, gpt, group, state), (0, 3, 1, 2)).reshape(kt, state, LANES)

    in_arrays = (row(lam_re), row(lam_im), row(ldt), b_cp(b_re), b_cp(b_im))
    out_arrays = (col(lam_re), col(lam_im), col(ldt), c_pc(c_re), c_pc(c_im))

    def in_specs(tile_of):
        row_spec = pl.BlockSpec((1, 1, ns), lambda *g: (tile_of(*g), 0, 0))
        b_spec = pl.BlockSpec((1, group, ns), lambda *g: (tile_of(*g), 0, 0))
        return [row_spec, row_spec, row_spec, b_spec, b_spec]

    def out_specs(tile_of):
        return [pl.BlockSpec((1, state, LANES), lambda *g: (tile_of(*g), 0, 0))] * 5

    dims = dict(kt=kt, ns=ns, group=group, state=state, chunk=SSM_CHUNK)
    return in_arrays, out_arrays, in_specs, out_specs, dims


def _chunk_rows(u_ref, rows, chunk):
    return [u_ref[pl.ds(j, rows, stride=chunk), :] for j in range(chunk)]


def _batch_rows(first, count, batch):
    return pl.ds(first, count) if batch == 1 else pl.ds(first, count, stride=batch)


def _ssm_state_kernel(u_ref, lr_ref, li_ref, ldt_ref, bre_ref, bim_ref, e_ref, a_ref, pe_ref, *,
                      chunk, batch, group, state):
    n_slabs, n_rows = e_ref.shape[1], e_ref.shape[2]
    n_chunks = n_rows // batch

    def store_pe(m, blk):
        j = chunk - 1 - m
        pe_ref[j * LANES:(j + 1) * LANES, :] = blk

    pr, pi = _ssm_in_blocks((lr_ref, li_ref, ldt_ref, bre_ref, bim_ref), chunk, store_pe,
                            group=group, state=state)
    a_ref[0] = jnp.concatenate([pr, pi], axis=1)

    u = jnp.concatenate(_chunk_rows(u_ref, n_rows, chunk), axis=1).astype(BF16)
    e = jnp.dot(u, pe_ref[...], preferred_element_type=F32)
    for b in range(batch):
        for s in range(n_slabs):
            e_ref[0, s, _batch_rows(b, n_chunks, batch), :] = (
                e[b * n_chunks:(b + 1) * n_chunks, s * LANES:(s + 1) * LANES])


def _ssm_state(proj, in_arrays, in_specs, dims, *, u_col0, batch):
    t = proj.shape[0]
    kt, ns, chunk = dims["kt"], dims["ns"], dims["chunk"]
    n_rows = t // chunk
    n_slabs = 2 * ns // LANES
    return pl.pallas_call(
        functools.partial(_ssm_state_kernel, chunk=chunk, batch=batch, group=dims["group"], state=dims["state"]),
        grid=(kt,),
        in_specs=[pl.BlockSpec((t, LANES), lambda k: (0, u_col0 + k))] + in_specs(lambda k: k),
        out_specs=[
            pl.BlockSpec((1, n_slabs, n_rows, LANES), lambda k: (k, 0, 0, 0)),
            pl.BlockSpec((1, 1, 2 * ns), lambda k: (k, 0, 0)),
        ],
        out_shape=[
            jax.ShapeDtypeStruct((kt, n_slabs, n_rows, LANES), F32),
            jax.ShapeDtypeStruct((kt, 1, 2 * ns), F32),
        ],
        scratch_shapes=[pltpu.VMEM((chunk * LANES, 2 * ns), BF16)],
        compiler_params=_params("parallel"),
        name="ssm_state",
    )(proj, *in_arrays)


def _ssm_scan_kernel(e_ref, a_ref, x_ref, *, batch):
    n_slabs, n_rows = e_ref.shape[1], e_ref.shape[2]
    half = n_slabs // 2
    n_chunks = n_rows // batch
    ar = jnp.broadcast_to(a_ref[0, :half], (half, batch, LANES))
    ai = jnp.broadcast_to(a_ref[0, half:], (half, batch, LANES))

    def step(n, carry):
        xr, xi = carry
        rows = pl.ds(n * batch, batch)
        x_ref[0, :half, rows, :] = xr
        x_ref[0, half:, rows, :] = xi
        er = e_ref[0, :half, rows, :]
        ei = e_ref[0, half:, rows, :]
        return ar * xr - ai * xi + er, ar * xi + ai * xr + ei

    zero = jnp.zeros((half, batch, LANES), F32)
    lax.fori_loop(0, n_chunks, step, (zero, zero))


def _ssm_scan(e, a_chunk, *, batch):
    kt, n_slabs, n_rows, _ = e.shape
    spec = pl.BlockSpec((1, n_slabs, n_rows, LANES), lambda k: (k, 0, 0, 0))
    return pl.pallas_call(
        functools.partial(_ssm_scan_kernel, batch=batch),
        grid=(kt,),
        in_specs=[spec, pl.BlockSpec((1, n_slabs, 1, LANES), lambda k: (k, 0, 0, 0))],
        out_specs=spec,
        out_shape=jax.ShapeDtypeStruct(e.shape, F32),
        compiler_params=_params("parallel"),
        name="ssm_scan",
    )(e, a_chunk.reshape(kt, n_slabs, 1, LANES))


def _ssm_out_kernel(u_ref, x_ref, *refs, chunk, batch, seqs, group, state):
    in_refs, out_refs, (d_ref, o_ref, mq_ref) = refs[:5], refs[5:10], refs[10:]
    i = pl.program_id(1)

    @pl.when(i == 0)
    def _():
        _ssm_out_matrix(in_refs, out_refs, mq_ref, chunk=chunk, group=group, state=state)

    rows = u_ref.shape[0] // chunk
    n_chunks = rows // seqs
    n_slabs = x_ref.shape[1]
    ns2 = n_slabs * LANES
    us = _chunk_rows(u_ref, rows, chunk)
    x = jnp.concatenate(
        [jnp.concatenate([x_ref[0, sl, _batch_rows(i * seqs + s, n_chunks, batch), :] for sl in range(n_slabs)],
                         axis=1) for s in range(seqs)], axis=0)
    xu = jnp.concatenate([x] + us, axis=1).astype(BF16)
    d = d_ref[0]
    pair = 2 * LANES
    def finish(c, y):
        for j in (c // LANES, c // LANES + 1):
            yj = y[:, j * LANES - c:(j + 1) * LANES - c] + d * us[j]
            o_ref[pl.ds(j, rows, stride=chunk), :] = jax.nn.gelu(yj)

    pending = None
    for c in range(0, chunk * LANES, pair):
        k_rows = ns2 + c + pair
        y = jnp.dot(xu[:, :k_rows], mq_ref[:k_rows, c:c + pair], preferred_element_type=F32)
        if pending is not None:
            finish(*pending)
        pending = (c, y)
    finish(*pending)


def _ssm_out(proj, x_prev, params, d_skip, *, u_col0, batch, seqs):
    in_arrays, out_arrays, in_specs, out_specs, dims = params
    t = proj.shape[0]
    kt, ns, chunk = dims["kt"], dims["ns"], dims["chunk"]
    _, n_slabs, n_rows, _ = x_prev.shape
    cl = chunk * LANES
    assert batch % seqs == 0 and chunk % 2 == 0 and n_slabs * LANES == 2 * ns
    tokens = t // batch * seqs
    tile = lambda k, i: k
    return pl.pallas_call(
        functools.partial(_ssm_out_kernel, chunk=chunk, batch=batch, seqs=seqs,
                          group=dims["group"], state=dims["state"]),
        grid=(kt, batch // seqs),
        in_specs=[
            pl.BlockSpec((tokens, LANES), lambda k, i: (i, u_col0 + k)),
            pl.BlockSpec((1, n_slabs, n_rows, LANES), lambda k, i: (k, 0, 0, 0)),
        ] + in_specs(tile) + out_specs(tile) + [pl.BlockSpec((1, 1, LANES), lambda k, i: (k, 0, 0))],
        out_specs=pl.BlockSpec((tokens, LANES), lambda k, i: (i, k)),
        out_shape=jax.ShapeDtypeStruct((t, kt * LANES), F32),
        scratch_shapes=[pltpu.VMEM((2 * ns + cl, cl), BF16)],
        compiler_params=_params("parallel", "arbitrary"),
        name="ssm_out",
    )(proj, x_prev, *in_arrays, *out_arrays, d_skip.astype(F32).reshape(kt, 1, LANES))


def _outproj_kernel(h_ref, a_ref, y_ref, gw_ref, gb_ref, wa_ref, wy_ref, o_ref):
    y = y_ref[...]
    z = jnp.dot(y.astype(BF16), gw_ref[...], preferred_element_type=F32) + gb_ref[...]
    yg = (y * jax.nn.sigmoid(z)).astype(BF16)
    mixed = jnp.dot(a_ref[...], wa_ref[...], preferred_element_type=F32)
    mixed = mixed + jnp.dot(yg, wy_ref[...], preferred_element_type=F32)
    o_ref[...] = h_ref[...] + mixed


def _outproj(h, attn, y, glu_w, glu_b, w_out, *, tm=512):
    t, d = h.shape
    wa = attn.shape[1]
    wy = y.shape[1]
    tm = _tile(t, tm)
    const = lambda shape: pl.BlockSpec(shape, lambda i: (0, 0))
    return pl.pallas_call(
        _outproj_kernel,
        grid=(t // tm,),
        in_specs=[
            pl.BlockSpec((tm, d), lambda i: (i, 0)),
            pl.BlockSpec((tm, wa), lambda i: (i, 0)),
            pl.BlockSpec((tm, wy), lambda i: (i, 0)),
            const((wy, wy)), const((1, wy)),
            pl.BlockSpec((wa, d), lambda i: (0, 0)),
            pl.BlockSpec((wy, d), lambda i: (wa // wy, 0)),
        ],
        out_specs=pl.BlockSpec((tm, d), lambda i: (i, 0)),
        out_shape=jax.ShapeDtypeStruct((t, d), F32),
        compiler_params=_params("parallel"),
        name="outproj",
    )(h, attn, y, glu_w, glu_b.astype(F32).reshape(1, wy), w_out, w_out)


def kernel(x, ffn1_norm, ffn1_w_gate, ffn1_w_up, ffn1_w_down, mix_norm, w_in, q_norm, k_norm, rel_bias,
           ssm_lambda_re, ssm_lambda_im, ssm_log_dt, ssm_b_re, ssm_b_im, ssm_c_re, ssm_c_im, ssm_d,
           glu_w, glu_b, w_out, ffn2_norm, ffn2_w_gate, ffn2_w_up, ffn2_w_down):
    batch, seq, d_model = x.shape
    depth = w_in.shape[0]
    head_dim = q_norm.shape[-1]
    n_heads = rel_bias.shape[1]
    attn_width = n_heads * head_dim
    ssm_width = glu_w.shape[-1]
    assert w_in.shape[-1] == 3 * attn_width + ssm_width and attn_width % ssm_width == 0
    assert ssm_width % LANES == 0 and seq % SSM_CHUNK == 0
    u_col0 = 3 * attn_width // LANES

    bias = _bias_tables(rel_bias)
    h = x.reshape(batch * seq, d_model).astype(F32)
    for l in range(depth):
        h = _ffn(h, ffn1_norm[l], ffn1_w_gate[l], ffn1_w_up[l], ffn1_w_down[l])
        proj = _proj(h, mix_norm[l], w_in[l].astype(BF16), q_norm[l], k_norm[l], attn_width=attn_width)
        attn = _attention(proj, bias, batch=batch, seq=seq, n_heads=n_heads, head_dim=head_dim)
        ssm = _ssm_params(ssm_lambda_re[l], ssm_lambda_im[l], ssm_log_dt[l],
                          ssm_b_re[l], ssm_b_im[l], ssm_c_re[l], ssm_c_im[l])
        e, a_chunk = _ssm_state(proj, ssm[0], ssm[2], ssm[4], u_col0=u_col0, batch=batch)
        x_prev = _ssm_scan(e, a_chunk, batch=batch)
        y = _ssm_out(proj, x_prev, ssm, ssm_d[l], u_col0=u_col0, batch=batch, seqs=2 if batch % 2 == 0 else 1)
        h = _outproj(h, attn, y, glu_w[l].astype(BF16), glu_b[l], w_out[l].astype(BF16))
        h = _ffn(h, ffn2_norm[l], ffn2_w_gate[l], ffn2_w_up[l], ffn2_w_down[l])
    return h.reshape(batch, seq, d_model).astype(x.dtype)
```

```python
import functools
import math

import jax
import jax.numpy as jnp
from jax import lax
from jax.experimental import pallas as pl
from jax.experimental.pallas import tpu as pltpu

F32 = jnp.float32
BF16 = jnp.bfloat16

EPS = 1e-6
NEG_INF = -1e30
DILATED_PATTERNS = ((128, 1), (512, 4), (2048, 16))
MAX_DISTANCE = 2048
SSM_CHUNK = 16
LANES = 128
VMEM_LIMIT_BYTES = 60 * 1024 * 1024


def _params(*semantics, flags=None):
    return pltpu.CompilerParams(dimension_semantics=semantics, vmem_limit_bytes=VMEM_LIMIT_BYTES, flags=flags)


def _tile(total, pref):
    if total <= pref:
        return total
    t = pref - pref % LANES
    while t > 0:
        if total % t == 0:
            return t
        t -= LANES
    return total


def _rms_norm(x, g):
    return x * lax.rsqrt(jnp.mean(x * x, axis=-1, keepdims=True) + EPS) * g


def _ffn_kernel(x_hbm, g_ref, wg_ref, wu_ref, wd_ref, o_ref, xn_ref, x_buf, x_sem):
    i, j = pl.program_id(0), pl.program_id(1)
    tm = x_buf.shape[0]

    def x_copy(tile):
        return pltpu.make_async_copy(x_hbm.at[pl.ds(tile * tm, tm), :], x_buf, x_sem)

    @pl.when((i == 0) & (j == 0))
    def _():
        x_copy(0).start()

    @pl.when(j == 0)
    def _():
        x_copy(i).wait()
        x = x_buf[...]
        xn_ref[...] = _rms_norm(x, g_ref[...]).astype(BF16)
        o_ref[...] = x

    @pl.when((j == 1) & (i + 1 < pl.num_programs(0)))
    def _():
        x_copy(i + 1).start()

    xn = xn_ref[...]
    tf = wg_ref.shape[1]
    half = tf // 2
    hs = []
    pending = None
    for c in (0, half):
        a = jnp.dot(xn, wg_ref[:, c:c + half].astype(BF16), preferred_element_type=F32)
        b = jnp.dot(xn, wu_ref[:, c:c + half].astype(BF16), preferred_element_type=F32)
        if pending is not None:
            pa, pb = pending
            hs.append((0.5 * pa * jax.nn.sigmoid(pa) * pb).astype(BF16))
        pending = (a, b)
    pa, pb = pending
    hs.append((0.5 * pa * jax.nn.sigmoid(pa) * pb).astype(BF16))
    h = jnp.concatenate(hs, axis=1)
    o_ref[...] += jnp.dot(h, wd_ref[...].astype(BF16), preferred_element_type=F32)


def _ffn(x, gain, wg, wu, wd, *, tm=1024, tf=512):
    t, d = x.shape
    f = wg.shape[1]
    tm = _tile(t, tm)
    tf = _tile(f, tf)
    assert f // tf >= 2, "the x prefetch is issued at the second hidden tile"
    return pl.pallas_call(
        _ffn_kernel,
        grid=(t // tm, f // tf),
        in_specs=[
            pl.BlockSpec(memory_space=pl.ANY),
            pl.BlockSpec((1, d), lambda i, j: (0, 0)),
            pl.BlockSpec((d, tf), lambda i, j: (0, j)),
            pl.BlockSpec((d, tf), lambda i, j: (0, j)),
            pl.BlockSpec((tf, d), lambda i, j: (j, 0)),
        ],
        out_specs=pl.BlockSpec((tm, d), lambda i, j: (i, 0)),
        out_shape=jax.ShapeDtypeStruct((t, d), F32),
        scratch_shapes=[pltpu.VMEM((tm, d), BF16), pltpu.VMEM((tm, d), F32), pltpu.SemaphoreType.DMA],
        compiler_params=_params("arbitrary", "arbitrary"),
        name="ffn",
    )(x, gain.reshape(1, d), wg, wu, wd)


def _proj_kernel(h_ref, g_ref, w_ref, qkg_ref, o_ref, hn_ref, *, n_norm_tiles, head_dim):
    n = pl.program_id(1)

    @pl.when(n == 0)
    def _():
        hn_ref[...] = _rms_norm(h_ref[...], g_ref[...]).astype(BF16)

    hn = hn_ref[...]
    normed_tile = n < n_norm_tiles
    chunk = 2 * head_dim

    def finish(c, acc):
        for h0 in range(0, chunk, head_dim):
            x = acc[:, h0:h0 + head_dim]
            cols = slice(c + h0, c + h0 + head_dim)
            o_ref[:, cols] = jnp.where(normed_tile, _rms_norm(x, qkg_ref[0, :, cols]), x)

    pending = None
    for c in range(0, w_ref.shape[1], chunk):
        acc = jnp.dot(hn, w_ref[:, c:c + chunk].astype(BF16), preferred_element_type=F32)
        if pending is not None:
            finish(*pending)
        pending = (c, acc)
    finish(*pending)


def _proj(h, gain, w_in, q_gain, k_gain, *, attn_width, tm=1024):
    t, d = h.shape
    n_out = w_in.shape[1]
    head_dim = q_gain.shape[0]
    tn = attn_width
    assert n_out % tn == 0
    n_tiles = n_out // tn
    reps = tn // head_dim
    qkg = jnp.ones((n_tiles, 1, tn), F32)
    qkg = qkg.at[0, 0].set(jnp.tile(q_gain, reps)).at[1, 0].set(jnp.tile(k_gain, reps))
    tm = _tile(t, tm)
    return pl.pallas_call(
        functools.partial(_proj_kernel, n_norm_tiles=2, head_dim=head_dim),
        grid=(t // tm, n_tiles),
        in_specs=[
            pl.BlockSpec((tm, d), lambda i, n: (i, 0)),
            pl.BlockSpec((1, d), lambda i, n: (0, 0)),
            pl.BlockSpec((d, tn), lambda i, n: (0, n)),
            pl.BlockSpec((1, 1, tn), lambda i, n: (n, 0, 0)),
        ],
        out_specs=pl.BlockSpec((tm, tn), lambda i, n: (i, n)),
        out_shape=jax.ShapeDtypeStruct((t, n_out), F32),
        scratch_shapes=[pltpu.VMEM((tm, d), BF16)],
        compiler_params=_params("parallel", "arbitrary"),
        name="proj",
    )(h, gain.reshape(1, d), w_in, qkg)


def _t5_bucket_tables(n_buckets):
    tables = []
    max_exact = n_buckets // 2
    for window, dilation in DILATED_PATTERNS:
        span = window // dilation
        qi = jnp.arange(span)[:, None]
        kj = jnp.arange(2 * span)[None, :]
        delta = qi + span - kj
        dist = jnp.maximum(delta, 0) * dilation
        d_f = jnp.maximum(dist, max_exact).astype(F32)
        large = max_exact + (jnp.log(d_f / max_exact) / math.log(MAX_DISTANCE / max_exact)
                             * (n_buckets - max_exact)).astype(jnp.int32)
        large = jnp.minimum(large, n_buckets - 1)
        bucket = jnp.where(dist < max_exact, dist, large)
        valid = (delta >= 0) & (delta <= span)
        tables.append(jnp.where(valid, bucket, -1).astype(jnp.int32))
    return jnp.stack(tables)


def _bias_kernel(rb_ref, bucket_ref, o_ref, *, n_buckets):
    h = pl.program_id(0)
    for p in range(bucket_ref.shape[0]):
        bk = bucket_ref[p]
        acc = jnp.full(bk.shape, NEG_INF, F32)
        for b in range(n_buckets):
            acc = jnp.where(bk == b, rb_ref[b, h] * math.log2(math.e), acc)
        o_ref[0, 2 * p] = acc
        in_prev_half = lax.broadcasted_iota(jnp.int32, bk.shape, 1) < bk.shape[1] // 2
        o_ref[0, 2 * p + 1] = jnp.where(in_prev_half, NEG_INF, acc)


def _bias_tables(rel_bias):
    n_buckets, n_heads = rel_bias.shape
    buckets = _t5_bucket_tables(n_buckets)
    n_pat, span, span2 = buckets.shape
    return pl.pallas_call(
        functools.partial(_bias_kernel, n_buckets=n_buckets),
        grid=(n_heads,),
        in_specs=[
            pl.BlockSpec(memory_space=pltpu.SMEM),
            pl.BlockSpec((n_pat, span, span2), lambda h: (0, 0, 0)),
        ],
        out_specs=pl.BlockSpec((1, 2 * n_pat, span, span2), lambda h: (h, 0, 0, 0)),
        out_shape=jax.ShapeDtypeStruct((n_heads, 2 * n_pat, span, span2), F32),
        compiler_params=_params("arbitrary"),
        name="bias",
    )(rel_bias.astype(F32), buckets)


ATTN_UNROLL = ((32, 16), (32, 4), (32, 4))


def _attn_kernel(q_ref, k_ref, v_ref, bias_ref, o_ref,
                 acc_a, m_a, l_a, acc_b, m_b, l_b, q4_ref, k4_ref, v4_ref, *, seq, span):
    hd = q_ref.shape[1]
    scale2 = math.log2(math.e) / math.sqrt(hd)
    nt_dims = (((1,), (1,)), ((), ()))
    g4, g16 = seq // 4, seq // 16
    state_a, state_b = (acc_a, m_a, l_a), (acc_b, m_b, l_b)
    src_nat, src_4 = (q_ref, k_ref, v_ref), (q4_ref, k4_ref, v4_ref)

    def run(n_classes, per_class, shape, src, stride, pat, state_in, state_out, save4, start_of, out_of):
        unroll, skew = shape
        assert per_class % unroll == 0 or unroll % per_class == 0
        total = n_classes * per_class
        assert total % unroll == 0

        def ld(ref, first_row):
            if stride == 1:
                return ref[pl.ds(first_row, span), :]
            return ref[pl.ds(first_row, span, stride=stride), :]

        def body(t, carry):
            blocks = []
            for u in range(unroll):
                if unroll % per_class == 0:
                    c, n = t * (unroll // per_class) + u // per_class, u % per_class
                    prev = "reuse" if n > 0 else None
                else:
                    trips_per_class = per_class // unroll
                    c, n = t // trips_per_class, (t % trips_per_class) * unroll + u
                    prev = "reuse" if u > 0 else "load"
                blocks.append(dict(n=n, prev=prev, start=start_of(c, n), rows_out=pl.ds(out_of(c, n), span)))

            def front(first):
                group = blocks[first:first + skew]
                for u, blk in enumerate(group, first):
                    q32, k32, v32 = (ld(r, blk["start"]) for r in src)
                    if save4:
                        q4_ref[blk["rows_out"], :] = q32
                        k4_ref[blk["rows_out"], :] = k32
                        v4_ref[blk["rows_out"], :] = v32
                    blk["q"], blk["k"], blk["v"] = q32.astype(BF16), k32.astype(BF16), v32.astype(BF16)
                    if blk["prev"] is None:
                        blk["kk"], blk["vv"] = blk["k"], blk["v"]
                        blk["bias"] = (2 * pat, True)
                        continue
                    if blk["prev"] == "load":
                        has_prev = jnp.minimum(blk["n"], 1)
                        first_row = blk["start"] - span * stride * has_prev
                        k_prev, v_prev = ld(src[1], first_row).astype(BF16), ld(src[2], first_row).astype(BF16)
                        blk["bias"] = (2 * pat + 1 - has_prev, False)
                    else:
                        k_prev, v_prev = blocks[u - 1]["k"], blocks[u - 1]["v"]
                        blk["bias"] = (2 * pat, False)
                    blk["kk"] = jnp.concatenate([k_prev, blk["k"]], axis=0)
                    blk["vv"] = jnp.concatenate([v_prev, blk["v"]], axis=0)
                for blk in group:
                    blk["s"] = lax.dot_general(blk["q"], blk["kk"], nt_dims, preferred_element_type=F32)

            def middle(first):
                group = blocks[first:first + skew]
                for blk in group:
                    table, cur_half_only = blk["bias"]
                    bias = bias_ref[0, table]
                    s = blk["s"] * scale2 + (bias[:, span:] if cur_half_only else bias)
                    m_blk = jnp.max(s, axis=-1, keepdims=True)
                    if state_in is None:
                        m_new = jnp.broadcast_to(m_blk, (span, hd))
                    else:
                        blk["m_old"] = ld(state_in[1], blk["start"])
                        m_new = jnp.maximum(blk["m_old"], m_blk)
                    m_wide = m_new if s.shape[1] == hd else jnp.concatenate([m_new, m_new], axis=1)
                    blk["p"] = jnp.exp2(s - m_wide).astype(BF16)
                    blk["m_new"] = m_new

            def back(first):
                group = blocks[first:first + skew]
                for blk in group:
                    v_aug = jnp.concatenate([blk["vv"], jnp.ones_like(blk["vv"])], axis=1)
                    blk["pv"] = jnp.dot(blk["p"], v_aug, preferred_element_type=F32)
                for blk in group:
                    o_new, l_new = blk["pv"][:, :hd], blk["pv"][:, hd:]
                    if state_in is not None:
                        alpha = jnp.exp2(blk["m_old"] - blk["m_new"])
                        o_new = alpha * ld(state_in[0], blk["start"]) + o_new
                        l_new = alpha * ld(state_in[2], blk["start"]) + l_new
                    state_out[0][blk["rows_out"], :] = o_new
                    state_out[1][blk["rows_out"], :] = blk["m_new"]
                    state_out[2][blk["rows_out"], :] = l_new

            front(0)
            for first in range(0, unroll, skew):
                if first + skew < unroll:
                    front(first + skew)
                middle(first)
                back(first)
            return carry

        lax.fori_loop(0, total // unroll, body, 0)

    blk4 = 4 * span

    run(1, seq // span, ATTN_UNROLL[0], src_nat, 1, 0, None, state_a, False,
        lambda c, n: n * span, lambda c, n: n * span)

    run(4, g4 // span, ATTN_UNROLL[1], src_nat, 4, 1, state_a, state_b, True,
        lambda c, n: n * blk4 + c, lambda c, n: c * g4 + n * span)

    run(16, g16 // span, ATTN_UNROLL[2], src_4, 4, 2, state_b, state_a, False,
        lambda c, n: (c // 4) * g4 + n * blk4 + c % 4,
        lambda c, n: (c // 4) * g4 + (c % 4) * g16 + n * span)

    for c in range(16):
        rows = pl.ds((c // 4) * g4 + (c % 4) * g16, g16)
        acc_b[pl.ds((c // 4) * g4 + c % 4, g16, stride=4), :] = acc_a[rows, :] / l_a[rows, :]
    for b in range(4):
        acc_a[pl.ds(b, g4, stride=4), :] = acc_b[pl.ds(b * g4, g4), :]
    o_ref[...] = acc_a[...].astype(o_ref.dtype)


def _attention(proj, bias, *, batch, seq, n_heads, head_dim):
    assert DILATED_PATTERNS == ((128, 1), (512, 4), (2048, 16)), "row orders are built for dilations 1, 4, 16"
    span = 128
    assert seq % DILATED_PATTERNS[-1][0] == 0
    n_tables = bias.shape[1]
    qkv_spec = lambda off: pl.BlockSpec((seq, head_dim), lambda b, h: (b, off + h))
    return pl.pallas_call(
        functools.partial(_attn_kernel, seq=seq, span=span),
        grid=(batch, n_heads),
        in_specs=[
            qkv_spec(0), qkv_spec(n_heads), qkv_spec(2 * n_heads),
            pl.BlockSpec((1, n_tables, span, 2 * span), lambda b, h: (h, 0, 0, 0)),
        ],
        out_specs=pl.BlockSpec((seq, head_dim), lambda b, h: (b, h)),
        out_shape=jax.ShapeDtypeStruct((batch * seq, n_heads * head_dim), BF16),
        scratch_shapes=[pltpu.VMEM((seq, head_dim), F32)] * 9,
        compiler_params=_params("parallel", "arbitrary"),
        name="attn",
    )(proj, proj, proj, bias)


def _ssm_discretize(lr, li, ldt):
    dt = jnp.exp(ldt)
    mag = jnp.exp(lr * dt)
    return mag * jnp.cos(li * dt), mag * jnp.sin(li * dt)


def _ssm_in_blocks(in_refs, n_powers, store, *, group, state):
    lr_ref, li_ref, ldt_ref, bre_ref, bim_ref = in_refs
    gpt = LANES // group
    ns = gpt * state
    lr, li = lr_ref[0], li_ref[0]
    ar, ai = _ssm_discretize(lr, li, ldt_ref[0])
    zr, zi = ar - 1.0, ai
    lam_sq = lr * lr + li * li
    coef_re = (zr * lr + zi * li) / lam_sq
    coef_im = (zi * lr - zr * li) / lam_sq
    bre, bim = bre_ref[0], bim_ref[0]
    bbar_re = coef_re * bre - coef_im * bim
    bbar_im = coef_re * bim + coef_im * bre

    row_g = lax.broadcasted_iota(jnp.int32, (LANES, ns), 0) // group
    col_g = lax.broadcasted_iota(jnp.int32, (LANES, ns), 1) // state
    mask_in = row_g == col_g

    def expand_in(w):
        return jnp.where(mask_in, jnp.concatenate([w] * gpt, axis=0), 0.0)

    pr = jnp.ones_like(ar)
    pi = jnp.zeros_like(ai)
    for m in range(n_powers):
        w_re = pr * bbar_re - pi * bbar_im
        w_im = pr * bbar_im + pi * bbar_re
        store(m, jnp.concatenate([expand_in(w_re), expand_in(w_im)], axis=1).astype(BF16))
        pr, pi = pr * ar - pi * ai, pr * ai + pi * ar
    return pr, pi


def _ssm_out_matrix(in_refs, out_refs, mq_ref, *, chunk, group, state):
    lr_ref, li_ref, ldt_ref, cre_ref, cim_ref = out_refs
    gpt = LANES // group
    ns = gpt * state
    first = []
    _ssm_in_blocks(in_refs, 1, lambda m, blk: first.append(blk), group=group, state=state)
    pe0 = first[0]

    acr, aci = _ssm_discretize(lr_ref[0], li_ref[0], ldt_ref[0])
    cre, cim = cre_ref[0], cim_ref[0]
    row_g = lax.broadcasted_iota(jnp.int32, (ns, LANES), 0) // state
    col_g = lax.broadcasted_iota(jnp.int32, (ns, LANES), 1) // group
    mask_out = row_g == col_g

    def expand_out(z):
        return jnp.where(mask_out, jnp.concatenate([z] * gpt, axis=0), 0.0)

    qr = jnp.ones_like(acr)
    qi = jnp.zeros_like(aci)
    zeros = jnp.zeros((LANES, LANES), BF16)
    for m in range(chunk + 1):
        z_re = expand_out(cre * qr - cim * qi)
        z_im = expand_out(-(cre * qi + cim * qr))
        qc_m = jnp.concatenate([z_re, z_im], axis=0).astype(BF16)
        if m >= 1:
            mq_ref[:2 * ns, (m - 1) * LANES:m * LANES] = qc_m
        if m < chunk:
            bd = jnp.dot(pe0, qc_m, preferred_element_type=F32).astype(BF16)
            for j in range(chunk - m):
                mq_ref[2 * ns + j * LANES:2 * ns + (j + 1) * LANES, (j + m) * LANES:(j + m + 1) * LANES] = bd
            if m >= 1:
                for j in range(m, chunk):
                    mq_ref[2 * ns + j * LANES:2 * ns + (j + 1) * LANES, (j - m) * LANES:(j - m + 1) * LANES] = zeros
        qr, qi = qr * acr - qi * aci, qr * aci + qi * acr


def _ssm_params(lam_re, lam_im, log_dt, b_re, b_im, c_re, c_im):
    n_groups, state, group = b_re.shape
    gpt = LANES // group
    kt = n_groups // gpt
    ns = gpt * state

    def row(a):
        return a.astype(F32).reshape(kt, 1, ns)

    def col(a):
        return jnp.repeat(jnp.transpose(a.astype(F32).reshape(kt, gpt, state), (0, 2, 1)), group, axis=-1)

    ldt = jnp.broadcast_to(log_dt.astype(F32)[:, None], (n_groups, state))
    b_cp = lambda b: jnp.transpose(b.astype(F32), (2, 0, 1)).reshape(group, kt, ns).transpose(1, 0, 2)
    c_pc = lambda c: jnp.transpose(c.astype(F32).reshape(kt, gpt, group, state), (0, 3, 1, 2)).reshape(kt, state, LANES)

    in_arrays = (row(lam_re), row(lam_im), row(ldt), b_cp(b_re), b_cp(b_im))
    out_arrays = (col(lam_re), col(lam_im), col(ldt), c_pc(c_re), c_pc(c_im))

    def in_specs(tile_of):
        row_spec = pl.BlockSpec((1, 1, ns), lambda *g: (tile_of(*g), 0, 0))
        b_spec = pl.BlockSpec((1, group, ns), lambda *g: (tile_of(*g), 0, 0))
        return [row_spec, row_spec, row_spec, b_spec, b_spec]

    def out_specs(tile_of):
        return [pl.BlockSpec((1, state, LANES), lambda *g: (tile_of(*g), 0, 0))] * 5

    dims = dict(kt=kt, ns=ns, group=group, state=state, chunk=SSM_CHUNK)
    return in_arrays, out_arrays, in_specs, out_specs, dims


def _chunk_rows(u_ref, rows, chunk):
    return [u_ref[pl.ds(j, rows, stride=chunk), :] for j in range(chunk)]


def _batch_rows(first, count, batch):
    return pl.ds(first, count) if batch == 1 else pl.ds(first, count, stride=batch)


def _ssm_state_kernel(u_ref, lr_ref, li_ref, ldt_ref, bre_ref, bim_ref, x_ref, pe_ref, *,
                      chunk, batch, group, state):
    n_slabs, n_rows = x_ref.shape[1], x_ref.shape[2]
    half = n_slabs // 2
    n_chunks = n_rows // batch

    def store_pe(m, blk):
        j = chunk - 1 - m
        pe_ref[j * LANES:(j + 1) * LANES, :] = blk

    pr, pi = _ssm_in_blocks((lr_ref, li_ref, ldt_ref, bre_ref, bim_ref), chunk, store_pe,
                            group=group, state=state)

    u = jnp.concatenate(_chunk_rows(u_ref, n_rows, chunk), axis=1).astype(BF16)
    e = jnp.dot(u, pe_ref[...], preferred_element_type=F32)
    for b in range(batch):
        for s in range(n_slabs):
            x_ref[0, s, _batch_rows(b, n_chunks, batch), :] = (
                e[b * n_chunks:(b + 1) * n_chunks, s * LANES:(s + 1) * LANES])

    def slabs(a):
        a = jnp.stack([a[:, s * LANES:(s + 1) * LANES] for s in range(half)])
        return jnp.broadcast_to(a, (half, batch, LANES))

    ar, ai = slabs(pr), slabs(pi)

    def step(n, carry):
        xr, xi = carry
        rows = pl.ds(n * batch, batch)
        er = x_ref[0, :half, rows, :]
        ei = x_ref[0, half:, rows, :]
        x_ref[0, :half, rows, :] = xr
        x_ref[0, half:, rows, :] = xi
        return ar * xr - ai * xi + er, ar * xi + ai * xr + ei

    zero = jnp.zeros((half, batch, LANES), F32)
    lax.fori_loop(0, n_chunks, step, (zero, zero))


def _ssm_state(proj, in_arrays, in_specs, dims, *, u_col0, batch):
    t = proj.shape[0]
    kt, ns, chunk = dims["kt"], dims["ns"], dims["chunk"]
    n_rows = t // chunk
    n_slabs = 2 * ns // LANES
    return pl.pallas_call(
        functools.partial(_ssm_state_kernel, chunk=chunk, batch=batch, group=dims["group"], state=dims["state"]),
        grid=(kt,),
        in_specs=[pl.BlockSpec((t, LANES), lambda k: (0, u_col0 + k))] + in_specs(lambda k: k),
        out_specs=pl.BlockSpec((1, n_slabs, n_rows, LANES), lambda k: (k, 0, 0, 0)),
        out_shape=jax.ShapeDtypeStruct((kt, n_slabs, n_rows, LANES), F32),
        scratch_shapes=[pltpu.VMEM((chunk * LANES, 2 * ns), BF16)],
        compiler_params=_params("parallel"),
        name="ssm_state",
    )(proj, *in_arrays)


def _ssm_out_kernel(u_ref, x_ref, *refs, chunk, batch, seqs, group, state):
    in_refs, out_refs, (d_ref, o_ref, mq_ref) = refs[:5], refs[5:10], refs[10:]
    i = pl.program_id(1)

    @pl.when(i == 0)
    def _():
        _ssm_out_matrix(in_refs, out_refs, mq_ref, chunk=chunk, group=group, state=state)

    rows = u_ref.shape[0] // chunk
    n_chunks = rows // seqs
    n_slabs = x_ref.shape[1]
    ns2 = n_slabs * LANES
    us = _chunk_rows(u_ref, rows, chunk)
    x = jnp.concatenate(
        [jnp.concatenate([x_ref[0, sl, _batch_rows(i * seqs + s, n_chunks, batch), :] for sl in range(n_slabs)],
                         axis=1) for s in range(seqs)], axis=0)
    xu = jnp.concatenate([x] + us, axis=1).astype(BF16)
    d = d_ref[0]
    pair = 2 * LANES
    def finish(c, y):
        for j in (c // LANES, c // LANES + 1):
            yj = y[:, j * LANES - c:(j + 1) * LANES - c] + d * us[j]
            o_ref[pl.ds(j, rows, stride=chunk), :] = jax.nn.gelu(yj)

    pending = None
    for c in range(0, chunk * LANES, pair):
        k_rows = ns2 + c + pair
        y = jnp.dot(xu[:, :k_rows], mq_ref[:k_rows, c:c + pair], preferred_element_type=F32)
        if pending is not None:
            finish(*pending)
        pending = (c, y)
    finish(*pending)


def _ssm_out(proj, x_prev, params, d_skip, *, u_col0, batch, seqs):
    in_arrays, out_arrays, in_specs, out_specs, dims = params
    t = proj.shape[0]
    kt, ns, chunk = dims["kt"], dims["ns"], dims["chunk"]
    _, n_slabs, n_rows, _ = x_prev.shape
    cl = chunk * LANES
    assert batch % seqs == 0 and chunk % 2 == 0 and n_slabs * LANES == 2 * ns
    tokens = t // batch * seqs
    tile = lambda k, i: k
    return pl.pallas_call(
        functools.partial(_ssm_out_kernel, chunk=chunk, batch=batch, seqs=seqs,
                          group=dims["group"], state=dims["state"]),
        grid=(kt, batch // seqs),
        in_specs=[
            pl.BlockSpec((tokens, LANES), lambda k, i: (i, u_col0 + k)),
            pl.BlockSpec((1, n_slabs, n_rows, LANES), lambda k, i: (k, 0, 0, 0)),
        ] + in_specs(tile) + out_specs(tile) + [pl.BlockSpec((1, 1, LANES), lambda k, i: (k, 0, 0))],
        out_specs=pl.BlockSpec((tokens, LANES), lambda k, i: (i, k)),
        out_shape=jax.ShapeDtypeStruct((t, kt * LANES), F32),
        scratch_shapes=[pltpu.VMEM((2 * ns + cl, cl), BF16)],
        compiler_params=_params("parallel", "arbitrary"),
        name="ssm_out",
    )(proj, x_prev, *in_arrays, *out_arrays, d_skip.astype(F32).reshape(kt, 1, LANES))


def _outproj_kernel(h_ref, a_ref, y_ref, gw_ref, gb_ref, wa_ref, wy_ref, o_ref, gw_bf, wa_bf, wy_bf):
    @pl.when(pl.program_id(0) == 0)
    def _():
        gw_bf[...] = gw_ref[...].astype(BF16)
        wa_bf[...] = wa_ref[...].astype(BF16)
        wy_bf[...] = wy_ref[...].astype(BF16)

    y = y_ref[...]
    z = jnp.dot(y.astype(BF16), gw_bf[...], preferred_element_type=F32) + gb_ref[...]
    yg = (y * jax.nn.sigmoid(z)).astype(BF16)
    mixed = jnp.dot(a_ref[...], wa_bf[...], preferred_element_type=F32)
    mixed = mixed + jnp.dot(yg, wy_bf[...], preferred_element_type=F32)
    o_ref[...] = h_ref[...] + mixed


def _outproj(h, attn, y, glu_w, glu_b, w_out, *, tm=512):
    t, d = h.shape
    wa = attn.shape[1]
    wy = y.shape[1]
    tm = _tile(t, tm)
    once = dict(pipeline_mode=pl.Buffered(1))
    return pl.pallas_call(
        _outproj_kernel,
        grid=(t // tm,),
        in_specs=[
            pl.BlockSpec((tm, d), lambda i: (i, 0)),
            pl.BlockSpec((tm, wa), lambda i: (i, 0)),
            pl.BlockSpec((tm, wy), lambda i: (i, 0)),
            pl.BlockSpec((wy, wy), lambda i: (0, 0), **once),
            pl.BlockSpec((1, wy), lambda i: (0, 0)),
            pl.BlockSpec((wa, d), lambda i: (0, 0), **once),
            pl.BlockSpec((wy, d), lambda i: (wa // wy, 0), **once),
        ],
        out_specs=pl.BlockSpec((tm, d), lambda i: (i, 0)),
        out_shape=jax.ShapeDtypeStruct((t, d), F32),
        scratch_shapes=[pltpu.VMEM((wy, wy), BF16), pltpu.VMEM((wa, d), BF16), pltpu.VMEM((wy, d), BF16)],
        compiler_params=_params("arbitrary"),
        name="outproj",
    )(h, attn, y, glu_w, glu_b.astype(F32).reshape(1, wy), w_out, w_out)


def kernel(x, ffn1_norm, ffn1_w_gate, ffn1_w_up, ffn1_w_down, mix_norm, w_in, q_norm, k_norm, rel_bias,
           ssm_lambda_re, ssm_lambda_im, ssm_log_dt, ssm_b_re, ssm_b_im, ssm_c_re, ssm_c_im, ssm_d,
           glu_w, glu_b, w_out, ffn2_norm, ffn2_w_gate, ffn2_w_up, ffn2_w_down):
    batch, seq, d_model = x.shape
    depth = w_in.shape[0]
    head_dim = q_norm.shape[-1]
    n_heads = rel_bias.shape[1]
    attn_width = n_heads * head_dim
    ssm_width = glu_w.shape[-1]
    assert w_in.shape[-1] == 3 * attn_width + ssm_width and attn_width % ssm_width == 0
    assert ssm_width % LANES == 0 and seq % SSM_CHUNK == 0
    u_col0 = 3 * attn_width // LANES

    bias = _bias_tables(rel_bias)
    h = x.reshape(batch * seq, d_model).astype(F32)
    for l in range(depth):
        h = _ffn(h, ffn1_norm[l], ffn1_w_gate[l], ffn1_w_up[l], ffn1_w_down[l])
        proj = _proj(h, mix_norm[l], w_in[l].astype(BF16), q_norm[l], k_norm[l], attn_width=attn_width)
        attn = _attention(proj, bias, batch=batch, seq=seq, n_heads=n_heads, head_dim=head_dim)
        ssm = _ssm_params(ssm_lambda_re[l], ssm_lambda_im[l], ssm_log_dt[l],
                          ssm_b_re[l], ssm_b_im[l], ssm_c_re[l], ssm_c_im[l])
        x_prev = _ssm_state(proj, ssm[0], ssm[2], ssm[4], u_col0=u_col0, batch=batch)
        y = _ssm_out(proj, x_prev, ssm, ssm_d[l], u_col0=u_col0, batch=batch, seqs=2 if batch % 2 == 0 else 1)
        h = _outproj(h, attn, y, glu_w[l], glu_b[l], w_out[l])
        h = _ffn(h, ffn2_norm[l], ffn2_w_gate[l], ffn2_w_up[l], ffn2_w_down[l])
    return h.reshape(batch, seq, d_model).astype(x.dtype)
```

```python
import functools
import math

import jax
import jax.numpy as jnp
from jax import lax
from jax.experimental import pallas as pl
from jax.experimental.pallas import tpu as pltpu

F32 = jnp.float32
BF16 = jnp.bfloat16

EPS = 1e-6
NEG_INF = -1e30
DILATED_PATTERNS = ((128, 1), (512, 4), (2048, 16))
MAX_DISTANCE = 2048
SSM_CHUNK = 16
LANES = 128
VMEM_LIMIT_BYTES = 60 * 1024 * 1024


def _params(*semantics, flags=None):
    return pltpu.CompilerParams(dimension_semantics=semantics, vmem_limit_bytes=VMEM_LIMIT_BYTES, flags=flags)


def _tile(total, pref):
    if total <= pref:
        return total
    t = pref - pref % LANES
    while t > 0:
        if total % t == 0:
            return t
        t -= LANES
    return total


def _rms_norm(x, g):
    return x * lax.rsqrt(jnp.mean(x * x, axis=-1, keepdims=True) + EPS) * g


def _ffn_kernel(x_hbm, g_ref, wg_ref, wu_ref, wd_ref, o_ref, xn_ref, x_buf, x_sem):
    i, j = pl.program_id(0), pl.program_id(1)
    tm = x_buf.shape[0]

    def x_copy(tile):
        return pltpu.make_async_copy(x_hbm.at[pl.ds(tile * tm, tm), :], x_buf, x_sem)

    @pl.when((i == 0) & (j == 0))
    def _():
        x_copy(0).start()

    @pl.when(j == 0)
    def _():
        x_copy(i).wait()
        x = x_buf[...]
        xn_ref[...] = _rms_norm(x, g_ref[...]).astype(BF16)
        o_ref[...] = x

    @pl.when((j == 1) & (i + 1 < pl.num_programs(0)))
    def _():
        x_copy(i + 1).start()

    xn = xn_ref[...]
    tf = wg_ref.shape[1]
    half = tf // 2
    hs = []
    pending = None
    for c in (0, half):
        a = jnp.dot(xn, wg_ref[:, c:c + half].astype(BF16), preferred_element_type=F32)
        b = jnp.dot(xn, wu_ref[:, c:c + half].astype(BF16), preferred_element_type=F32)
        if pending is not None:
            pa, pb = pending
            hs.append((0.5 * pa * jax.nn.sigmoid(pa) * pb).astype(BF16))
        pending = (a, b)
    pa, pb = pending
    hs.append((0.5 * pa * jax.nn.sigmoid(pa) * pb).astype(BF16))
    h = jnp.concatenate(hs, axis=1)
    o_ref[...] += jnp.dot(h, wd_ref[...].astype(BF16), preferred_element_type=F32)


def _ffn(x, gain, wg, wu, wd, *, tm=1024, tf=512):
    t, d = x.shape
    f = wg.shape[1]
    tm = _tile(t, tm)
    tf = _tile(f, tf)
    assert f // tf >= 2, "the x prefetch is issued at the second hidden tile"
    return pl.pallas_call(
        _ffn_kernel,
        grid=(t // tm, f // tf),
        in_specs=[
            pl.BlockSpec(memory_space=pl.ANY),
            pl.BlockSpec((1, d), lambda i, j: (0, 0)),
            pl.BlockSpec((d, tf), lambda i, j: (0, j)),
            pl.BlockSpec((d, tf), lambda i, j: (0, j)),
            pl.BlockSpec((tf, d), lambda i, j: (j, 0)),
        ],
        out_specs=pl.BlockSpec((tm, d), lambda i, j: (i, 0)),
        out_shape=jax.ShapeDtypeStruct((t, d), F32),
        scratch_shapes=[pltpu.VMEM((tm, d), BF16), pltpu.VMEM((tm, d), F32), pltpu.SemaphoreType.DMA],
        compiler_params=_params("arbitrary", "arbitrary"),
        name="ffn",
    )(x, gain.reshape(1, d), wg, wu, wd)


def _proj_kernel(h_ref, g_ref, w_ref, qkg_ref, o_ref, hn_ref, *, n_norm_tiles, head_dim):
    n = pl.program_id(1)

    @pl.when(n == 0)
    def _():
        hn_ref[...] = _rms_norm(h_ref[...], g_ref[...]).astype(BF16)

    hn = hn_ref[...]
    normed_tile = n < n_norm_tiles
    chunk = 2 * head_dim

    def finish(c, acc):
        for h0 in range(0, chunk, head_dim):
            x = acc[:, h0:h0 + head_dim]
            cols = slice(c + h0, c + h0 + head_dim)
            o_ref[:, cols] = jnp.where(normed_tile, _rms_norm(x, qkg_ref[0, :, cols]), x)

    pending = None
    for c in range(0, w_ref.shape[1], chunk):
        acc = jnp.dot(hn, w_ref[:, c:c + chunk].astype(BF16), preferred_element_type=F32)
        if pending is not None:
            finish(*pending)
        pending = (c, acc)
    finish(*pending)


def _proj(h, gain, w_in, q_gain, k_gain, *, attn_width, tm=1024):
    t, d = h.shape
    n_out = w_in.shape[1]
    head_dim = q_gain.shape[0]
    tn = 2 * attn_width
    assert n_out % tn == 0
    n_tiles = n_out // tn
    reps = attn_width // head_dim
    qk_gain = jnp.concatenate([jnp.tile(q_gain, reps), jnp.tile(k_gain, reps)]).astype(F32)
    qkg = jnp.ones((n_tiles, 1, tn), F32).at[0, 0].set(qk_gain)
    tm = _tile(t, tm)
    return pl.pallas_call(
        functools.partial(_proj_kernel, n_norm_tiles=1, head_dim=head_dim),
        grid=(t // tm, n_tiles),
        in_specs=[
            pl.BlockSpec((tm, d), lambda i, n: (i, 0)),
            pl.BlockSpec((1, d), lambda i, n: (0, 0)),
            pl.BlockSpec((d, tn), lambda i, n: (0, n)),
            pl.BlockSpec((1, 1, tn), lambda i, n: (n, 0, 0)),
        ],
        out_specs=pl.BlockSpec((tm, tn), lambda i, n: (i, n)),
        out_shape=jax.ShapeDtypeStruct((t, n_out), F32),
        scratch_shapes=[pltpu.VMEM((tm, d), BF16)],
        compiler_params=_params("parallel", "arbitrary"),
        name="proj",
    )(h, gain.reshape(1, d), w_in, qkg)


def _t5_bucket_tables(n_buckets):
    tables = []
    max_exact = n_buckets // 2
    for window, dilation in DILATED_PATTERNS:
        span = window // dilation
        qi = jnp.arange(span)[:, None]
        kj = jnp.arange(2 * span)[None, :]
        delta = qi + span - kj
        dist = jnp.maximum(delta, 0) * dilation
        d_f = jnp.maximum(dist, max_exact).astype(F32)
        large = max_exact + (jnp.log(d_f / max_exact) / math.log(MAX_DISTANCE / max_exact)
                             * (n_buckets - max_exact)).astype(jnp.int32)
        large = jnp.minimum(large, n_buckets - 1)
        bucket = jnp.where(dist < max_exact, dist, large)
        valid = (delta >= 0) & (delta <= span)
        tables.append(jnp.where(valid, bucket, -1).astype(jnp.int32))
    return jnp.stack(tables)


def _bias_kernel(rb_ref, bucket_ref, o_ref, *, n_buckets):
    h = pl.program_id(0)
    for p in range(bucket_ref.shape[0]):
        bk = bucket_ref[p]
        acc = jnp.full(bk.shape, NEG_INF, F32)
        for b in range(n_buckets):
            acc = jnp.where(bk == b, rb_ref[b, h] * math.log2(math.e), acc)
        o_ref[0, 2 * p] = acc
        in_prev_half = lax.broadcasted_iota(jnp.int32, bk.shape, 1) < bk.shape[1] // 2
        o_ref[0, 2 * p + 1] = jnp.where(in_prev_half, NEG_INF, acc)


def _bias_tables(rel_bias):
    n_buckets, n_heads = rel_bias.shape
    buckets = _t5_bucket_tables(n_buckets)
    n_pat, span, span2 = buckets.shape
    return pl.pallas_call(
        functools.partial(_bias_kernel, n_buckets=n_buckets),
        grid=(n_heads,),
        in_specs=[
            pl.BlockSpec(memory_space=pltpu.SMEM),
            pl.BlockSpec((n_pat, span, span2), lambda h: (0, 0, 0)),
        ],
        out_specs=pl.BlockSpec((1, 2 * n_pat, span, span2), lambda h: (h, 0, 0, 0)),
        out_shape=jax.ShapeDtypeStruct((n_heads, 2 * n_pat, span, span2), F32),
        compiler_params=_params("arbitrary"),
        name="bias",
    )(rel_bias.astype(F32), buckets)


ATTN_UNROLL = ((32, 16), (32, 4), (32, 4))


def _attn_kernel(q_ref, k_ref, v_ref, bias_ref, o_ref,
                 acc_a, m_a, l_a, acc_b, m_b, l_b, q4_ref, k4_ref, v4_ref, *, seq, span):
    hd = q_ref.shape[1]
    scale2 = math.log2(math.e) / math.sqrt(hd)
    nt_dims = (((1,), (1,)), ((), ()))
    g4, g16 = seq // 4, seq // 16
    state_a, state_b = (acc_a, m_a, l_a), (acc_b, m_b, l_b)
    src_nat, src_4 = (q_ref, k_ref, v_ref), (q4_ref, k4_ref, v4_ref)

    def run(n_classes, per_class, shape, src, stride, pat, state_in, state_out, save4, start_of, out_of):
        unroll, skew = shape
        assert per_class % unroll == 0 or unroll % per_class == 0
        total = n_classes * per_class
        assert total % unroll == 0

        def ld(ref, first_row):
            if stride == 1:
                return ref[pl.ds(first_row, span), :]
            return ref[pl.ds(first_row, span, stride=stride), :]

        def body(t, carry):
            blocks = []
            for u in range(unroll):
                if unroll % per_class == 0:
                    c, n = t * (unroll // per_class) + u // per_class, u % per_class
                    prev = "reuse" if n > 0 else None
                else:
                    trips_per_class = per_class // unroll
                    c, n = t // trips_per_class, (t % trips_per_class) * unroll + u
                    prev = "reuse" if u > 0 else "load"
                blocks.append(dict(n=n, prev=prev, start=start_of(c, n), rows_out=pl.ds(out_of(c, n), span)))

            def front(first):
                group = blocks[first:first + skew]
                for u, blk in enumerate(group, first):
                    q32, k32, v32 = (ld(r, blk["start"]) for r in src)
                    if save4:
                        q4_ref[blk["rows_out"], :] = q32
                        k4_ref[blk["rows_out"], :] = k32
                        v4_ref[blk["rows_out"], :] = v32
                    blk["q"], blk["k"], blk["v"] = q32.astype(BF16), k32.astype(BF16), v32.astype(BF16)
                    if blk["prev"] is None:
                        blk["kk"], blk["vv"] = blk["k"], blk["v"]
                        blk["bias"] = (2 * pat, True)
                        continue
                    if blk["prev"] == "load":
                        has_prev = jnp.minimum(blk["n"], 1)
                        first_row = blk["start"] - span * stride * has_prev
                        k_prev, v_prev = ld(src[1], first_row).astype(BF16), ld(src[2], first_row).astype(BF16)
                        blk["bias"] = (2 * pat + 1 - has_prev, False)
                    else:
                        k_prev, v_prev = blocks[u - 1]["k"], blocks[u - 1]["v"]
                        blk["bias"] = (2 * pat, False)
                    blk["kk"] = jnp.concatenate([k_prev, blk["k"]], axis=0)
                    blk["vv"] = jnp.concatenate([v_prev, blk["v"]], axis=0)
                for blk in group:
                    blk["s"] = lax.dot_general(blk["q"], blk["kk"], nt_dims, preferred_element_type=F32)

            def middle(first):
                group = blocks[first:first + skew]
                for blk in group:
                    table, cur_half_only = blk["bias"]
                    bias = bias_ref[0, table]
                    s = blk["s"] * scale2 + (bias[:, span:] if cur_half_only else bias)
                    m_blk = jnp.max(s, axis=-1, keepdims=True)
                    if state_in is None:
                        m_new = jnp.broadcast_to(m_blk, (span, hd))
                    else:
                        blk["m_old"] = ld(state_in[1], blk["start"])
                        m_new = jnp.maximum(blk["m_old"], m_blk)
                    m_wide = m_new if s.shape[1] == hd else jnp.concatenate([m_new, m_new], axis=1)
                    blk["p"] = jnp.exp2(s - m_wide).astype(BF16)
                    blk["m_new"] = m_new

            def back(first):
                group = blocks[first:first + skew]
                for blk in group:
                    v_aug = jnp.concatenate([blk["vv"], jnp.ones_like(blk["vv"])], axis=1)
                    blk["pv"] = jnp.dot(blk["p"], v_aug, preferred_element_type=F32)
                for blk in group:
                    o_new, l_new = blk["pv"][:, :hd], blk["pv"][:, hd:]
                    if state_in is not None:
                        alpha = jnp.exp2(blk["m_old"] - blk["m_new"])
                        o_new = alpha * ld(state_in[0], blk["start"]) + o_new
                        l_new = alpha * ld(state_in[2], blk["start"]) + l_new
                    state_out[0][blk["rows_out"], :] = o_new
                    state_out[1][blk["rows_out"], :] = blk["m_new"]
                    state_out[2][blk["rows_out"], :] = l_new

            front(0)
            for first in range(0, unroll, skew):
                if first + skew < unroll:
                    front(first + skew)
                middle(first)
                back(first)
            return carry

        lax.fori_loop(0, total // unroll, body, 0)

    blk4 = 4 * span

    run(1, seq // span, ATTN_UNROLL[0], src_nat, 1, 0, None, state_a, False,
        lambda c, n: n * span, lambda c, n: n * span)

    run(4, g4 // span, ATTN_UNROLL[1], src_nat, 4, 1, state_a, state_b, True,
        lambda c, n: n * blk4 + c, lambda c, n: c * g4 + n * span)

    run(16, g16 // span, ATTN_UNROLL[2], src_4, 4, 2, state_b, state_a, False,
        lambda c, n: (c // 4) * g4 + n * blk4 + c % 4,
        lambda c, n: (c // 4) * g4 + (c % 4) * g16 + n * span)

    for c in range(16):
        rows = pl.ds((c // 4) * g4 + (c % 4) * g16, g16)
        acc_b[pl.ds((c // 4) * g4 + c % 4, g16, stride=4), :] = acc_a[rows, :] / l_a[rows, :]
    for b in range(4):
        acc_a[pl.ds(b, g4, stride=4), :] = acc_b[pl.ds(b * g4, g4), :]
    o_ref[...] = acc_a[...].astype(o_ref.dtype)


def _attention(proj, bias, *, batch, seq, n_heads, head_dim):
    assert DILATED_PATTERNS == ((128, 1), (512, 4), (2048, 16)), "row orders are built for dilations 1, 4, 16"
    span = 128
    assert seq % DILATED_PATTERNS[-1][0] == 0
    n_tables = bias.shape[1]
    qkv_spec = lambda off: pl.BlockSpec((seq, head_dim), lambda b, h: (b, off + h))
    return pl.pallas_call(
        functools.partial(_attn_kernel, seq=seq, span=span),
        grid=(batch, n_heads),
        in_specs=[
            qkv_spec(0), qkv_spec(n_heads), qkv_spec(2 * n_heads),
            pl.BlockSpec((1, n_tables, span, 2 * span), lambda b, h: (h, 0, 0, 0)),
        ],
        out_specs=pl.BlockSpec((seq, head_dim), lambda b, h: (b, h)),
        out_shape=jax.ShapeDtypeStruct((batch * seq, n_heads * head_dim), BF16),
        scratch_shapes=[pltpu.VMEM((seq, head_dim), F32)] * 9,
        compiler_params=_params("parallel", "arbitrary"),
        name="attn",
    )(proj, proj, proj, bias)


def _ssm_discretize(lr, li, ldt):
    dt = jnp.exp(ldt)
    mag = jnp.exp(lr * dt)
    return mag * jnp.cos(li * dt), mag * jnp.sin(li * dt)


def _ssm_in_blocks(in_refs, n_powers, store, *, group, state):
    lr_ref, li_ref, ldt_ref, bre_ref, bim_ref = in_refs
    gpt = LANES // group
    ns = gpt * state
    lr, li = lr_ref[0], li_ref[0]
    ar, ai = _ssm_discretize(lr, li, ldt_ref[0])
    zr, zi = ar - 1.0, ai
    lam_sq = lr * lr + li * li
    coef_re = (zr * lr + zi * li) / lam_sq
    coef_im = (zi * lr - zr * li) / lam_sq
    bre, bim = bre_ref[0], bim_ref[0]
    bbar_re = coef_re * bre - coef_im * bim
    bbar_im = coef_re * bim + coef_im * bre

    row_g = lax.broadcasted_iota(jnp.int32, (LANES, ns), 0) // group
    col_g = lax.broadcasted_iota(jnp.int32, (LANES, ns), 1) // state
    mask_in = row_g == col_g

    def expand_in(w):
        return jnp.where(mask_in, jnp.concatenate([w] * gpt, axis=0), 0.0)

    pr = jnp.ones_like(ar)
    pi = jnp.zeros_like(ai)
    for m in range(n_powers):
        w_re = pr * bbar_re - pi * bbar_im
        w_im = pr * bbar_im + pi * bbar_re
        store(m, jnp.concatenate([expand_in(w_re), expand_in(w_im)], axis=1).astype(BF16))
        pr, pi = pr * ar - pi * ai, pr * ai + pi * ar
    return pr, pi


def _ssm_out_matrix(in_refs, out_refs, mq_ref, *, chunk, group, state):
    lr_ref, li_ref, ldt_ref, cre_ref, cim_ref = out_refs
    gpt = LANES // group
    ns = gpt * state
    first = []
    _ssm_in_blocks(in_refs, 1, lambda m, blk: first.append(blk), group=group, state=state)
    pe0 = first[0]

    acr, aci = _ssm_discretize(lr_ref[0], li_ref[0], ldt_ref[0])
    cre, cim = cre_ref[0], cim_ref[0]
    row_g = lax.broadcasted_iota(jnp.int32, (ns, LANES), 0) // state
    col_g = lax.broadcasted_iota(jnp.int32, (ns, LANES), 1) // group
    mask_out = row_g == col_g

    def expand_out(z):
        return jnp.where(mask_out, jnp.concatenate([z] * gpt, axis=0), 0.0)

    qr = jnp.ones_like(acr)
    qi = jnp.zeros_like(aci)
    zeros = jnp.zeros((LANES, LANES), BF16)
    for m in range(chunk + 1):
        z_re = expand_out(cre * qr - cim * qi)
        z_im = expand_out(-(cre * qi + cim * qr))
        qc_m = jnp.concatenate([z_re, z_im], axis=0).astype(BF16)
        if m >= 1:
            mq_ref[:2 * ns, (m - 1) * LANES:m * LANES] = qc_m
        if m < chunk:
            bd = jnp.dot(pe0, qc_m, preferred_element_type=F32).astype(BF16)
            for j in range(chunk - m):
                mq_ref[2 * ns + j * LANES:2 * ns + (j + 1) * LANES, (j + m) * LANES:(j + m + 1) * LANES] = bd
            if m >= 1:
                for j in range(m, chunk):
                    mq_ref[2 * ns + j * LANES:2 * ns + (j + 1) * LANES, (j - m) * LANES:(j - m + 1) * LANES] = zeros
        qr, qi = qr * acr - qi * aci, qr * aci + qi * acr


def _ssm_params(lam_re, lam_im, log_dt, b_re, b_im, c_re, c_im):
    n_groups, state, group = b_re.shape
    gpt = LANES // group
    kt = n_groups // gpt
    ns = gpt * state

    def row(a):
        return a.astype(F32).reshape(kt, 1, ns)

    def col(a):
        return jnp.repeat(jnp.transpose(a.astype(F32).reshape(kt, gpt, state), (0, 2, 1)), group, axis=-1)

    ldt = jnp.broadcast_to(log_dt.astype(F32)[:, None], (n_groups, state))
    b_cp = lambda b: jnp.transpose(b.astype(F32), (2, 0, 1)).reshape(group, kt, ns).transpose(1, 0, 2)
    c_pc = lambda c: jnp.transpose(c.astype(F32).reshape(kt, gpt, group, state), (0, 3, 1, 2)).reshape(kt, state, LANES)

    in_arrays = (row(lam_re), row(lam_im), row(ldt), b_cp(b_re), b_cp(b_im))
    out_arrays = (col(lam_re), col(lam_im), col(ldt), c_pc(c_re), c_pc(c_im))

    def in_specs(tile_of):
        row_spec = pl.BlockSpec((1, 1, ns), lambda *g: (tile_of(*g), 0, 0))
        b_spec = pl.BlockSpec((1, group, ns), lambda *g: (tile_of(*g), 0, 0))
        return [row_spec, row_spec, row_spec, b_spec, b_spec]

    def out_specs(tile_of):
        return [pl.BlockSpec((1, state, LANES), lambda *g: (tile_of(*g), 0, 0))] * 5

    dims = dict(kt=kt, ns=ns, group=group, state=state, chunk=SSM_CHUNK)
    return in_arrays, out_arrays, in_specs, out_specs, dims


def _chunk_rows(u_ref, rows, chunk):
    return [u_ref[pl.ds(j, rows, stride=chunk), :] for j in range(chunk)]


def _batch_rows(first, count, batch):
    return pl.ds(first, count) if batch == 1 else pl.ds(first, count, stride=batch)


def _ssm_state_kernel(u_ref, lr_ref, li_ref, ldt_ref, bre_ref, bim_ref, x_ref, pe_ref, *,
                      chunk, batch, group, state):
    n_slabs, n_rows = x_ref.shape[1], x_ref.shape[2]
    half = n_slabs // 2
    n_chunks = n_rows // batch

    def store_pe(m, blk):
        j = chunk - 1 - m
        pe_ref[j * LANES:(j + 1) * LANES, :] = blk

    pr, pi = _ssm_in_blocks((lr_ref, li_ref, ldt_ref, bre_ref, bim_ref), chunk, store_pe,
                            group=group, state=state)

    u = jnp.concatenate(_chunk_rows(u_ref, n_rows, chunk), axis=1).astype(BF16)
    e = jnp.dot(u, pe_ref[...], preferred_element_type=F32)
    for b in range(batch):
        for s in range(n_slabs):
            x_ref[0, s, _batch_rows(b, n_chunks, batch), :] = (
                e[b * n_chunks:(b + 1) * n_chunks, s * LANES:(s + 1) * LANES])

    def slabs(a):
        a = jnp.stack([a[:, s * LANES:(s + 1) * LANES] for s in range(half)])
        return jnp.broadcast_to(a, (half, batch, LANES))

    ar, ai = slabs(pr), slabs(pi)

    def step(n, carry):
        xr, xi = carry
        rows = pl.ds(n * batch, batch)
        er = x_ref[0, :half, rows, :]
        ei = x_ref[0, half:, rows, :]
        x_ref[0, :half, rows, :] = xr
        x_ref[0, half:, rows, :] = xi
        return ar * xr - ai * xi + er, ar * xi + ai * xr + ei

    zero = jnp.zeros((half, batch, LANES), F32)
    lax.fori_loop(0, n_chunks, step, (zero, zero))


def _ssm_state(proj, in_arrays, in_specs, dims, *, u_col0, batch):
    t = proj.shape[0]
    kt, ns, chunk = dims["kt"], dims["ns"], dims["chunk"]
    n_rows = t // chunk
    n_slabs = 2 * ns // LANES
    return pl.pallas_call(
        functools.partial(_ssm_state_kernel, chunk=chunk, batch=batch, group=dims["group"], state=dims["state"]),
        grid=(kt,),
        in_specs=[pl.BlockSpec((t, LANES), lambda k: (0, u_col0 + k))] + in_specs(lambda k: k),
        out_specs=pl.BlockSpec((1, n_slabs, n_rows, LANES), lambda k: (k, 0, 0, 0)),
        out_shape=jax.ShapeDtypeStruct((kt, n_slabs, n_rows, LANES), F32),
        scratch_shapes=[pltpu.VMEM((chunk * LANES, 2 * ns), BF16)],
        compiler_params=_params("parallel"),
        name="ssm_state",
    )(proj, *in_arrays)


def _ssm_out_kernel(u_ref, x_ref, *refs, chunk, batch, seqs, group, state):
    in_refs, out_refs, (d_ref, o_ref, mq_ref) = refs[:5], refs[5:10], refs[10:]
    i = pl.program_id(1)

    @pl.when(i == 0)
    def _():
        _ssm_out_matrix(in_refs, out_refs, mq_ref, chunk=chunk, group=group, state=state)

    rows = u_ref.shape[0] // chunk
    n_chunks = rows // seqs
    n_slabs = x_ref.shape[1]
    ns2 = n_slabs * LANES
    us = _chunk_rows(u_ref, rows, chunk)
    x = jnp.concatenate(
        [jnp.concatenate([x_ref[0, sl, _batch_rows(i * seqs + s, n_chunks, batch), :] for sl in range(n_slabs)],
                         axis=1) for s in range(seqs)], axis=0)
    xu = jnp.concatenate([x] + us, axis=1).astype(BF16)
    d = d_ref[0]
    pair = 2 * LANES
    def finish(c, y):
        for j in (c // LANES, c // LANES + 1):
            yj = y[:, j * LANES - c:(j + 1) * LANES - c] + d * us[j]
            o_ref[pl.ds(j, rows, stride=chunk), :] = jax.nn.gelu(yj)

    pending = None
    for c in range(0, chunk * LANES, pair):
        k_rows = ns2 + c + pair
        y = jnp.dot(xu[:, :k_rows], mq_ref[:k_rows, c:c + pair], preferred_element_type=F32)
        if pending is not None:
            finish(*pending)
        pending = (c, y)
    finish(*pending)


def _ssm_out(proj, x_prev, params, d_skip, *, u_col0, batch, seqs):
    in_arrays, out_arrays, in_specs, out_specs, dims = params
    t = proj.shape[0]
    kt, ns, chunk = dims["kt"], dims["ns"], dims["chunk"]
    _, n_slabs, n_rows, _ = x_prev.shape
    cl = chunk * LANES
    assert batch % seqs == 0 and chunk % 2 == 0 and n_slabs * LANES == 2 * ns
    tokens = t // batch * seqs
    tile = lambda k, i: k
    return pl.pallas_call(
        functools.partial(_ssm_out_kernel, chunk=chunk, batch=batch, seqs=seqs,
                          group=dims["group"], state=dims["state"]),
        grid=(kt, batch // seqs),
        in_specs=[
            pl.BlockSpec((tokens, LANES), lambda k, i: (i, u_col0 + k)),
            pl.BlockSpec((1, n_slabs, n_rows, LANES), lambda k, i: (k, 0, 0, 0)),
        ] + in_specs(tile) + out_specs(tile) + [pl.BlockSpec((1, 1, LANES), lambda k, i: (k, 0, 0))],
        out_specs=pl.BlockSpec((tokens, LANES), lambda k, i: (i, k)),
        out_shape=jax.ShapeDtypeStruct((t, kt * LANES), F32),
        scratch_shapes=[pltpu.VMEM((2 * ns + cl, cl), BF16)],
        compiler_params=_params("parallel", "arbitrary"),
        name="ssm_out",
    )(proj, x_prev, *in_arrays, *out_arrays, d_skip.astype(F32).reshape(kt, 1, LANES))


def _outproj_kernel(h_ref, a_ref, y_ref, gw_ref, gb_ref, wa_ref, wy_ref, o_ref, gw_bf, wa_bf, wy_bf):
    @pl.when(pl.program_id(0) == 0)
    def _():
        gw_bf[...] = gw_ref[...].astype(BF16)
        wa_bf[...] = wa_ref[...].astype(BF16)
        wy_bf[...] = wy_ref[...].astype(BF16)

    y = y_ref[...]
    z = jnp.dot(y.astype(BF16), gw_bf[...], preferred_element_type=F32) + gb_ref[...]
    yg = (y * jax.nn.sigmoid(z)).astype(BF16)
    mixed = jnp.dot(a_ref[...], wa_bf[...], preferred_element_type=F32)
    mixed = mixed + jnp.dot(yg, wy_bf[...], preferred_element_type=F32)
    o_ref[...] = h_ref[...] + mixed


def _outproj(h, attn, y, glu_w, glu_b, w_out, *, tm=512):
    t, d = h.shape
    wa = attn.shape[1]
    wy = y.shape[1]
    tm = _tile(t, tm)
    once = dict(pipeline_mode=pl.Buffered(1))
    return pl.pallas_call(
        _outproj_kernel,
        grid=(t // tm,),
        in_specs=[
            pl.BlockSpec((tm, d), lambda i: (i, 0)),
            pl.BlockSpec((tm, wa), lambda i: (i, 0)),
            pl.BlockSpec((tm, wy), lambda i: (i, 0)),
            pl.BlockSpec((wy, wy), lambda i: (0, 0), **once),
            pl.BlockSpec((1, wy), lambda i: (0, 0)),
            pl.BlockSpec((wa, d), lambda i: (0, 0), **once),
            pl.BlockSpec((wy, d), lambda i: (wa // wy, 0), **once),
        ],
        out_specs=pl.BlockSpec((tm, d), lambda i: (i, 0)),
        out_shape=jax.ShapeDtypeStruct((t, d), F32),
        scratch_shapes=[pltpu.VMEM((wy, wy), BF16), pltpu.VMEM((wa, d), BF16), pltpu.VMEM((wy, d), BF16)],
        compiler_params=_params("arbitrary"),
        name="outproj",
    )(h, attn, y, glu_w, glu_b.astype(F32).reshape(1, wy), w_out, w_out)


def kernel(x, ffn1_norm, ffn1_w_gate, ffn1_w_up, ffn1_w_down, mix_norm, w_in, q_norm, k_norm, rel_bias,
           ssm_lambda_re, ssm_lambda_im, ssm_log_dt, ssm_b_re, ssm_b_im, ssm_c_re, ssm_c_im, ssm_d,
           glu_w, glu_b, w_out, ffn2_norm, ffn2_w_gate, ffn2_w_up, ffn2_w_down):
    batch, seq, d_model = x.shape
    depth = w_in.shape[0]
    head_dim = q_norm.shape[-1]
    n_heads = rel_bias.shape[1]
    attn_width = n_heads * head_dim
    ssm_width = glu_w.shape[-1]
    assert w_in.shape[-1] == 3 * attn_width + ssm_width and attn_width % ssm_width == 0
    assert ssm_width % LANES == 0 and seq % SSM_CHUNK == 0
    u_col0 = 3 * attn_width // LANES

    bias = _bias_tables(rel_bias)
    h = x.reshape(batch * seq, d_model).astype(F32)
    for l in range(depth):
        h = _ffn(h, ffn1_norm[l], ffn1_w_gate[l], ffn1_w_up[l], ffn1_w_down[l])
        proj = _proj(h, mix_norm[l], w_in[l].astype(BF16), q_norm[l], k_norm[l], attn_width=attn_width)
        attn = _attention(proj, bias, batch=batch, seq=seq, n_heads=n_heads, head_dim=head_dim)
        ssm = _ssm_params(ssm_lambda_re[l], ssm_lambda_im[l], ssm_log_dt[l],
                          ssm_b_re[l], ssm_b_im[l], ssm_c_re[l], ssm_c_im[l])
        x_prev = _ssm_state(proj, ssm[0], ssm[2], ssm[4], u_col0=u_col0, batch=batch)
        y = _ssm_out(proj, x_prev, ssm, ssm_d[l], u_col0=u_col0, batch=batch, seqs=2 if batch % 2 == 0 else 1)
        h = _outproj(h, attn, y, glu_w[l], glu_b[l], w_out[l])
        h = _ffn(h, ffn2_norm[l], ffn2_w_gate[l], ffn2_w_up[l], ffn2_w_down[l])
    return h.reshape(batch, seq, d_model).astype(x.dtype)
```

```python
import functools
import math

import jax
import jax.numpy as jnp
from jax import lax
from jax.experimental import pallas as pl
from jax.experimental.pallas import tpu as pltpu

F32 = jnp.float32
BF16 = jnp.bfloat16

EPS = 1e-6
NEG_INF = -1e30
DILATED_PATTERNS = ((128, 1), (512, 4), (2048, 16))
MAX_DISTANCE = 2048
SSM_CHUNK = 16
LANES = 128
VMEM_LIMIT_BYTES = 60 * 1024 * 1024


def _params(*semantics, flags=None):
    return pltpu.CompilerParams(dimension_semantics=semantics, vmem_limit_bytes=VMEM_LIMIT_BYTES, flags=flags)


def _tile(total, pref):
    if total <= pref:
        return total
    t = pref - pref % LANES
    while t > 0:
        if total % t == 0:
            return t
        t -= LANES
    return total


def _rms_norm(x, g):
    return x * lax.rsqrt(jnp.mean(x * x, axis=-1, keepdims=True) + EPS) * g


def _ffn_kernel(x_hbm, g_ref, wg_ref, wu_ref, wd_ref, o_ref, xn_ref, x_buf, x_sem):
    i, j = pl.program_id(0), pl.program_id(1)
    tm = x_buf.shape[0]

    def x_copy(tile):
        return pltpu.make_async_copy(x_hbm.at[pl.ds(tile * tm, tm), :], x_buf, x_sem)

    @pl.when((i == 0) & (j == 0))
    def _():
        x_copy(0).start()

    @pl.when(j == 0)
    def _():
        x_copy(i).wait()
        x = x_buf[...]
        xn_ref[...] = _rms_norm(x, g_ref[...]).astype(BF16)
        o_ref[...] = x

    @pl.when((j == 1) & (i + 1 < pl.num_programs(0)))
    def _():
        x_copy(i + 1).start()

    xn = xn_ref[...]
    tf = wg_ref.shape[1]
    half = tf // 2
    hs = []
    pending = None
    for c in (0, half):
        a = jnp.dot(xn, wg_ref[:, c:c + half].astype(BF16), preferred_element_type=F32)
        b = jnp.dot(xn, wu_ref[:, c:c + half].astype(BF16), preferred_element_type=F32)
        if pending is not None:
            pa, pb = pending
            hs.append((0.5 * pa * jax.nn.sigmoid(pa) * pb).astype(BF16))
        pending = (a, b)
    pa, pb = pending
    hs.append((0.5 * pa * jax.nn.sigmoid(pa) * pb).astype(BF16))
    h = jnp.concatenate(hs, axis=1)
    o_ref[...] += jnp.dot(h, wd_ref[...].astype(BF16), preferred_element_type=F32)


def _ffn(x, gain, wg, wu, wd, *, tm=1024, tf=512):
    t, d = x.shape
    f = wg.shape[1]
    tm = _tile(t, tm)
    tf = _tile(f, tf)
    assert f // tf >= 2, "the x prefetch is issued at the second hidden tile"
    return pl.pallas_call(
        _ffn_kernel,
        grid=(t // tm, f // tf),
        in_specs=[
            pl.BlockSpec(memory_space=pl.ANY),
            pl.BlockSpec((1, d), lambda i, j: (0, 0)),
            pl.BlockSpec((d, tf), lambda i, j: (0, j)),
            pl.BlockSpec((d, tf), lambda i, j: (0, j)),
            pl.BlockSpec((tf, d), lambda i, j: (j, 0)),
        ],
        out_specs=pl.BlockSpec((tm, d), lambda i, j: (i, 0)),
        out_shape=jax.ShapeDtypeStruct((t, d), F32),
        scratch_shapes=[pltpu.VMEM((tm, d), BF16), pltpu.VMEM((tm, d), F32), pltpu.SemaphoreType.DMA],
        compiler_params=_params("arbitrary", "arbitrary"),
        name="ffn",
    )(x, gain.reshape(1, d), wg, wu, wd)


PROJ_TM, PROJ_TN = 512, 4096


def _proj_kernel(h_ref, g_ref, w_ref, qkg_ref, o_ref, hn_ref, *, norm_cols, head_dim):
    n = pl.program_id(1)

    @pl.when(n == 0)
    def _():
        hn_ref[...] = _rms_norm(h_ref[...], g_ref[...]).astype(BF16)

    hn = hn_ref[...]
    first_col = n * w_ref.shape[1]
    chunk = 2 * head_dim

    def finish(c, acc):
        for h0 in range(0, chunk, head_dim):
            x = acc[:, h0:h0 + head_dim]
            cols = slice(c + h0, c + h0 + head_dim)
            normed = first_col + c + h0 < norm_cols
            o_ref[:, cols] = jnp.where(normed, _rms_norm(x, qkg_ref[0, :, cols]), x)

    pending = None
    for c in range(0, w_ref.shape[1], chunk):
        acc = jnp.dot(hn, w_ref[:, c:c + chunk].astype(BF16), preferred_element_type=F32)
        if pending is not None:
            finish(*pending)
        pending = (c, acc)
    finish(*pending)


def _proj(h, gain, w_in, q_gain, k_gain, *, attn_width, tm=PROJ_TM, tn=PROJ_TN):
    t, d = h.shape
    n_out = w_in.shape[1]
    head_dim = q_gain.shape[0]
    tn = min(tn, n_out)
    assert n_out % tn == 0 and (2 * attn_width) % tn in (0, 2 * attn_width)
    n_tiles = n_out // tn
    reps = attn_width // head_dim
    qk_gain = jnp.concatenate([jnp.tile(q_gain, reps), jnp.tile(k_gain, reps)]).astype(F32)
    qkg = jnp.ones((n_out,), F32).at[:2 * attn_width].set(qk_gain).reshape(n_tiles, 1, tn)
    tm = _tile(t, tm)
    w_mode = dict(pipeline_mode=pl.Buffered(1)) if n_tiles == 1 else {}
    return pl.pallas_call(
        functools.partial(_proj_kernel, norm_cols=2 * attn_width, head_dim=head_dim),
        grid=(t // tm, n_tiles),
        in_specs=[
            pl.BlockSpec((tm, d), lambda i, n: (i, 0)),
            pl.BlockSpec((1, d), lambda i, n: (0, 0)),
            pl.BlockSpec((d, tn), lambda i, n: (0, n), **w_mode),
            pl.BlockSpec((1, 1, tn), lambda i, n: (n, 0, 0)),
        ],
        out_specs=pl.BlockSpec((tm, tn), lambda i, n: (i, n)),
        out_shape=jax.ShapeDtypeStruct((t, n_out), F32),
        scratch_shapes=[pltpu.VMEM((tm, d), BF16)],
        compiler_params=_params("parallel", "arbitrary"),
        name="proj",
    )(h, gain.reshape(1, d), w_in, qkg)


def _t5_bucket_tables(n_buckets):
    tables = []
    max_exact = n_buckets // 2
    for window, dilation in DILATED_PATTERNS:
        span = window // dilation
        qi = jnp.arange(span)[:, None]
        kj = jnp.arange(2 * span)[None, :]
        delta = qi + span - kj
        dist = jnp.maximum(delta, 0) * dilation
        d_f = jnp.maximum(dist, max_exact).astype(F32)
        large = max_exact + (jnp.log(d_f / max_exact) / math.log(MAX_DISTANCE / max_exact)
                             * (n_buckets - max_exact)).astype(jnp.int32)
        large = jnp.minimum(large, n_buckets - 1)
        bucket = jnp.where(dist < max_exact, dist, large)
        valid = (delta >= 0) & (delta <= span)
        tables.append(jnp.where(valid, bucket, -1).astype(jnp.int32))
    return jnp.stack(tables)


def _bias_kernel(rb_ref, bucket_ref, o_ref, *, n_buckets):
    h = pl.program_id(0)
    for p in range(bucket_ref.shape[0]):
        bk = bucket_ref[p]
        acc = jnp.full(bk.shape, NEG_INF, F32)
        for b in range(n_buckets):
            acc = jnp.where(bk == b, rb_ref[b, h] * math.log2(math.e), acc)
        o_ref[0, 2 * p] = acc
        in_prev_half = lax.broadcasted_iota(jnp.int32, bk.shape, 1) < bk.shape[1] // 2
        o_ref[0, 2 * p + 1] = jnp.where(in_prev_half, NEG_INF, acc)


def _bias_tables(rel_bias):
    n_buckets, n_heads = rel_bias.shape
    buckets = _t5_bucket_tables(n_buckets)
    n_pat, span, span2 = buckets.shape
    return pl.pallas_call(
        functools.partial(_bias_kernel, n_buckets=n_buckets),
        grid=(n_heads,),
        in_specs=[
            pl.BlockSpec(memory_space=pltpu.SMEM),
            pl.BlockSpec((n_pat, span, span2), lambda h: (0, 0, 0)),
        ],
        out_specs=pl.BlockSpec((1, 2 * n_pat, span, span2), lambda h: (h, 0, 0, 0)),
        out_shape=jax.ShapeDtypeStruct((n_heads, 2 * n_pat, span, span2), F32),
        compiler_params=_params("arbitrary"),
        name="bias",
    )(rel_bias.astype(F32), buckets)


ATTN_UNROLL = ((32, 16), (32, 4), (32, 4))


def _attn_kernel(q_ref, k_ref, v_ref, bias_ref, o_ref,
                 acc_a, m_a, l_a, acc_b, m_b, l_b, q4_ref, k4_ref, v4_ref, *, seq, span):
    hd = q_ref.shape[1]
    scale2 = math.log2(math.e) / math.sqrt(hd)
    nt_dims = (((1,), (1,)), ((), ()))
    g4, g16 = seq // 4, seq // 16
    state_a, state_b = (acc_a, m_a, l_a), (acc_b, m_b, l_b)
    src_nat, src_4 = (q_ref, k_ref, v_ref), (q4_ref, k4_ref, v4_ref)

    def run(n_classes, per_class, shape, src, stride, pat, state_in, state_out, save4, start_of, out_of):
        unroll, skew = shape
        assert per_class % unroll == 0 or unroll % per_class == 0
        total = n_classes * per_class
        assert total % unroll == 0

        def ld(ref, first_row):
            if stride == 1:
                return ref[pl.ds(first_row, span), :]
            return ref[pl.ds(first_row, span, stride=stride), :]

        def body(t, carry):
            blocks = []
            for u in range(unroll):
                if unroll % per_class == 0:
                    c, n = t * (unroll // per_class) + u // per_class, u % per_class
                    prev = "reuse" if n > 0 else None
                else:
                    trips_per_class = per_class // unroll
                    c, n = t // trips_per_class, (t % trips_per_class) * unroll + u
                    prev = "reuse" if u > 0 else "load"
                blocks.append(dict(n=n, prev=prev, start=start_of(c, n), rows_out=pl.ds(out_of(c, n), span)))

            def front(first):
                group = blocks[first:first + skew]
                for u, blk in enumerate(group, first):
                    q32, k32, v32 = (ld(r, blk["start"]) for r in src)
                    if save4:
                        q4_ref[blk["rows_out"], :] = q32
                        k4_ref[blk["rows_out"], :] = k32
                        v4_ref[blk["rows_out"], :] = v32
                    blk["q"], blk["k"], blk["v"] = q32.astype(BF16), k32.astype(BF16), v32.astype(BF16)
                    if blk["prev"] is None:
                        blk["kk"], blk["vv"] = blk["k"], blk["v"]
                        blk["bias"] = (2 * pat, True)
                        continue
                    if blk["prev"] == "load":
                        has_prev = jnp.minimum(blk["n"], 1)
                        first_row = blk["start"] - span * stride * has_prev
                        k_prev, v_prev = ld(src[1], first_row).astype(BF16), ld(src[2], first_row).astype(BF16)
                        blk["bias"] = (2 * pat + 1 - has_prev, False)
                    else:
                        k_prev, v_prev = blocks[u - 1]["k"], blocks[u - 1]["v"]
                        blk["bias"] = (2 * pat, False)
                    blk["kk"] = jnp.concatenate([k_prev, blk["k"]], axis=0)
                    blk["vv"] = jnp.concatenate([v_prev, blk["v"]], axis=0)
                for blk in group:
                    blk["s"] = lax.dot_general(blk["q"], blk["kk"], nt_dims, preferred_element_type=F32)

            def middle(first):
                group = blocks[first:first + skew]
                for blk in group:
                    table, cur_half_only = blk["bias"]
                    bias = bias_ref[0, table]
                    s = blk["s"] * scale2 + (bias[:, span:] if cur_half_only else bias)
                    m_blk = jnp.max(s, axis=-1, keepdims=True)
                    if state_in is None:
                        m_new = jnp.broadcast_to(m_blk, (span, hd))
                    else:
                        blk["m_old"] = ld(state_in[1], blk["start"])
                        m_new = jnp.maximum(blk["m_old"], m_blk)
                    m_wide = m_new if s.shape[1] == hd else jnp.concatenate([m_new, m_new], axis=1)
                    blk["p"] = jnp.exp2(s - m_wide).astype(BF16)
                    blk["m_new"] = m_new

            def back(first):
                group = blocks[first:first + skew]
                for blk in group:
                    v_aug = jnp.concatenate([blk["vv"], jnp.ones_like(blk["vv"])], axis=1)
                    blk["pv"] = jnp.dot(blk["p"], v_aug, preferred_element_type=F32)
                for blk in group:
                    o_new, l_new = blk["pv"][:, :hd], blk["pv"][:, hd:]
                    if state_in is not None:
                        alpha = jnp.exp2(blk["m_old"] - blk["m_new"])
                        o_new = alpha * ld(state_in[0], blk["start"]) + o_new
                        l_new = alpha * ld(state_in[2], blk["start"]) + l_new
                    state_out[0][blk["rows_out"], :] = o_new
                    state_out[1][blk["rows_out"], :] = blk["m_new"]
                    state_out[2][blk["rows_out"], :] = l_new

            front(0)
            for first in range(0, unroll, skew):
                if first + skew < unroll:
                    front(first + skew)
                middle(first)
                back(first)
            return carry

        lax.fori_loop(0, total // unroll, body, 0)

    blk4 = 4 * span

    run(1, seq // span, ATTN_UNROLL[0], src_nat, 1, 0, None, state_a, False,
        lambda c, n: n * span, lambda c, n: n * span)

    run(4, g4 // span, ATTN_UNROLL[1], src_nat, 4, 1, state_a, state_b, True,
        lambda c, n: n * blk4 + c, lambda c, n: c * g4 + n * span)

    run(16, g16 // span, ATTN_UNROLL[2], src_4, 4, 2, state_b, state_a, False,
        lambda c, n: (c // 4) * g4 + n * blk4 + c % 4,
        lambda c, n: (c // 4) * g4 + (c % 4) * g16 + n * span)

    for c in range(16):
        rows = pl.ds((c // 4) * g4 + (c % 4) * g16, g16)
        acc_b[pl.ds((c // 4) * g4 + c % 4, g16, stride=4), :] = acc_a[rows, :] / l_a[rows, :]
    for b in range(4):
        acc_a[pl.ds(b, g4, stride=4), :] = acc_b[pl.ds(b * g4, g4), :]
    o_ref[...] = acc_a[...].astype(o_ref.dtype)


def _attention(proj, bias, *, batch, seq, n_heads, head_dim):
    assert DILATED_PATTERNS == ((128, 1), (512, 4), (2048, 16)), "row orders are built for dilations 1, 4, 16"
    span = 128
    assert seq % DILATED_PATTERNS[-1][0] == 0
    n_tables = bias.shape[1]
    qkv_spec = lambda off: pl.BlockSpec((seq, head_dim), lambda b, h: (b, off + h))
    return pl.pallas_call(
        functools.partial(_attn_kernel, seq=seq, span=span),
        grid=(batch, n_heads),
        in_specs=[
            qkv_spec(0), qkv_spec(n_heads), qkv_spec(2 * n_heads),
            pl.BlockSpec((1, n_tables, span, 2 * span), lambda b, h: (h, 0, 0, 0)),
        ],
        out_specs=pl.BlockSpec((seq, head_dim), lambda b, h: (b, h)),
        out_shape=jax.ShapeDtypeStruct((batch * seq, n_heads * head_dim), BF16),
        scratch_shapes=[pltpu.VMEM((seq, head_dim), F32)] * 9,
        compiler_params=_params("parallel", "arbitrary"),
        name="attn",
    )(proj, proj, proj, bias)


def _ssm_discretize(lr, li, ldt):
    dt = jnp.exp(ldt)
    mag = jnp.exp(lr * dt)
    return mag * jnp.cos(li * dt), mag * jnp.sin(li * dt)


def _ssm_in_blocks(in_refs, n_powers, store, *, group, state):
    lr_ref, li_ref, ldt_ref, bre_ref, bim_ref = in_refs
    gpt = LANES // group
    ns = gpt * state
    lr, li = lr_ref[0], li_ref[0]
    ar, ai = _ssm_discretize(lr, li, ldt_ref[0])
    zr, zi = ar - 1.0, ai
    lam_sq = lr * lr + li * li
    coef_re = (zr * lr + zi * li) / lam_sq
    coef_im = (zi * lr - zr * li) / lam_sq
    bre, bim = bre_ref[0], bim_ref[0]
    bbar_re = coef_re * bre - coef_im * bim
    bbar_im = coef_re * bim + coef_im * bre

    row_g = lax.broadcasted_iota(jnp.int32, (LANES, ns), 0) // group
    col_g = lax.broadcasted_iota(jnp.int32, (LANES, ns), 1) // state
    mask_in = row_g == col_g

    def expand_in(w):
        return jnp.where(mask_in, jnp.concatenate([w] * gpt, axis=0), 0.0)

    pr = jnp.ones_like(ar)
    pi = jnp.zeros_like(ai)
    for m in range(n_powers):
        w_re = pr * bbar_re - pi * bbar_im
        w_im = pr * bbar_im + pi * bbar_re
        store(m, jnp.concatenate([expand_in(w_re), expand_in(w_im)], axis=1).astype(BF16))
        pr, pi = pr * ar - pi * ai, pr * ai + pi * ar
    return pr, pi


def _ssm_out_matrix(in_refs, out_refs, mq_ref, *, chunk, group, state):
    lr_ref, li_ref, ldt_ref, cre_ref, cim_ref = out_refs
    gpt = LANES // group
    ns = gpt * state
    first = []
    _ssm_in_blocks(in_refs, 1, lambda m, blk: first.append(blk), group=group, state=state)
    pe0 = first[0]

    acr, aci = _ssm_discretize(lr_ref[0], li_ref[0], ldt_ref[0])
    cre, cim = cre_ref[0], cim_ref[0]
    row_g = lax.broadcasted_iota(jnp.int32, (ns, LANES), 0) // state
    col_g = lax.broadcasted_iota(jnp.int32, (ns, LANES), 1) // group
    mask_out = row_g == col_g

    def expand_out(z):
        return jnp.where(mask_out, jnp.concatenate([z] * gpt, axis=0), 0.0)

    qr = jnp.ones_like(acr)
    qi = jnp.zeros_like(aci)
    zeros = jnp.zeros((LANES, LANES), BF16)
    for m in range(chunk + 1):
        z_re = expand_out(cre * qr - cim * qi)
        z_im = expand_out(-(cre * qi + cim * qr))
        qc_m = jnp.concatenate([z_re, z_im], axis=0).astype(BF16)
        if m >= 1:
            mq_ref[:2 * ns, (m - 1) * LANES:m * LANES] = qc_m
        if m < chunk:
            bd = jnp.dot(pe0, qc_m, preferred_element_type=F32).astype(BF16)
            for j in range(chunk - m):
                mq_ref[2 * ns + j * LANES:2 * ns + (j + 1) * LANES, (j + m) * LANES:(j + m + 1) * LANES] = bd
            if m >= 1:
                for j in range(m, chunk):
                    mq_ref[2 * ns + j * LANES:2 * ns + (j + 1) * LANES, (j - m) * LANES:(j - m + 1) * LANES] = zeros
        qr, qi = qr * acr - qi * aci, qr * aci + qi * acr


def _ssm_params(lam_re, lam_im, log_dt, b_re, b_im, c_re, c_im):
    n_groups, state, group = b_re.shape
    gpt = LANES // group
    kt = n_groups // gpt
    ns = gpt * state

    def row(a):
        return a.astype(F32).reshape(kt, 1, ns)

    def col(a):
        return jnp.repeat(jnp.transpose(a.astype(F32).reshape(kt, gpt, state), (0, 2, 1)), group, axis=-1)

    ldt = jnp.broadcast_to(log_dt.astype(F32)[:, None], (n_groups, state))
    b_cp = lambda b: jnp.transpose(b.astype(F32), (2, 0, 1)).reshape(group, kt, ns).transpose(1, 0, 2)
    c_pc = lambda c: jnp.transpose(c.astype(F32).reshape(kt, gpt, group, state), (0, 3, 1, 2)).reshape(kt, state, LANES)

    in_arrays = (row(lam_re), row(lam_im), row(ldt), b_cp(b_re), b_cp(b_im))
    out_arrays = (col(lam_re), col(lam_im), col(ldt), c_pc(c_re), c_pc(c_im))

    def in_specs(tile_of):
        row_spec = pl.BlockSpec((1, 1, ns), lambda *g: (tile_of(*g), 0, 0))
        b_spec = pl.BlockSpec((1, group, ns), lambda *g: (tile_of(*g), 0, 0))
        return [row_spec, row_spec, row_spec, b_spec, b_spec]

    def out_specs(tile_of):
        return [pl.BlockSpec((1, state, LANES), lambda *g: (tile_of(*g), 0, 0))] * 5

    dims = dict(kt=kt, ns=ns, group=group, state=state, chunk=SSM_CHUNK)
    return in_arrays, out_arrays, in_specs, out_specs, dims


def _chunk_rows(u_ref, rows, chunk):
    return [u_ref[pl.ds(j, rows, stride=chunk), :] for j in range(chunk)]


def _batch_rows(first, count, batch):
    return pl.ds(first, count) if batch == 1 else pl.ds(first, count, stride=batch)


def _ssm_state_kernel(u_ref, lr_ref, li_ref, ldt_ref, bre_ref, bim_ref, x_ref, pe_ref, *,
                      chunk, batch, group, state):
    n_slabs, n_rows = x_ref.shape[1], x_ref.shape[2]
    half = n_slabs // 2
    n_chunks = n_rows // batch

    def store_pe(m, blk):
        j = chunk - 1 - m
        pe_ref[j * LANES:(j + 1) * LANES, :] = blk

    pr, pi = _ssm_in_blocks((lr_ref, li_ref, ldt_ref, bre_ref, bim_ref), chunk, store_pe,
                            group=group, state=state)

    u = jnp.concatenate(_chunk_rows(u_ref, n_rows, chunk), axis=1).astype(BF16)
    e = jnp.dot(u, pe_ref[...], preferred_element_type=F32)
    for b in range(batch):
        for s in range(n_slabs):
            x_ref[0, s, _batch_rows(b, n_chunks, batch), :] = (
                e[b * n_chunks:(b + 1) * n_chunks, s * LANES:(s + 1) * LANES])

    def slabs(a):
        a = jnp.stack([a[:, s * LANES:(s + 1) * LANES] for s in range(half)])
        return jnp.broadcast_to(a, (half, batch, LANES))

    ar, ai = slabs(pr), slabs(pi)

    def step(n, carry):
        xr, xi = carry
        rows = pl.ds(n * batch, batch)
        er = x_ref[0, :half, rows, :]
        ei = x_ref[0, half:, rows, :]
        x_ref[0, :half, rows, :] = xr
        x_ref[0, half:, rows, :] = xi
        return ar * xr - ai * xi + er, ar * xi + ai * xr + ei

    zero = jnp.zeros((half, batch, LANES), F32)
    lax.fori_loop(0, n_chunks, step, (zero, zero))


def _ssm_state(proj, in_arrays, in_specs, dims, *, u_col0, batch):
    t = proj.shape[0]
    kt, ns, chunk = dims["kt"], dims["ns"], dims["chunk"]
    n_rows = t // chunk
    n_slabs = 2 * ns // LANES
    return pl.pallas_call(
        functools.partial(_ssm_state_kernel, chunk=chunk, batch=batch, group=dims["group"], state=dims["state"]),
        grid=(kt,),
        in_specs=[pl.BlockSpec((t, LANES), lambda k: (0, u_col0 + k))] + in_specs(lambda k: k),
        out_specs=pl.BlockSpec((1, n_slabs, n_rows, LANES), lambda k: (k, 0, 0, 0)),
        out_shape=jax.ShapeDtypeStruct((kt, n_slabs, n_rows, LANES), F32),
        scratch_shapes=[pltpu.VMEM((chunk * LANES, 2 * ns), BF16)],
        compiler_params=_params("parallel"),
        name="ssm_state",
    )(proj, *in_arrays)


def _ssm_out_kernel(u_ref, x_ref, *refs, chunk, batch, seqs, group, state):
    in_refs, out_refs, (d_ref, o_ref, mq_ref) = refs[:5], refs[5:10], refs[10:]
    i = pl.program_id(1)

    @pl.when(i == 0)
    def _():
        _ssm_out_matrix(in_refs, out_refs, mq_ref, chunk=chunk, group=group, state=state)

    rows = u_ref.shape[0] // chunk
    n_chunks = rows // seqs
    n_slabs = x_ref.shape[1]
    ns2 = n_slabs * LANES
    us = _chunk_rows(u_ref, rows, chunk)
    x = jnp.concatenate(
        [jnp.concatenate([x_ref[0, sl, _batch_rows(i * seqs + s, n_chunks, batch), :] for sl in range(n_slabs)],
                         axis=1) for s in range(seqs)], axis=0)
    xu = jnp.concatenate([x] + us, axis=1).astype(BF16)
    d = d_ref[0]
    pair = 2 * LANES
    def finish(c, y):
        for j in (c // LANES, c // LANES + 1):
            yj = y[:, j * LANES - c:(j + 1) * LANES - c] + d * us[j]
            o_ref[pl.ds(j, rows, stride=chunk), :] = jax.nn.gelu(yj)

    pending = None
    for c in range(0, chunk * LANES, pair):
        k_rows = ns2 + c + pair
        y = jnp.dot(xu[:, :k_rows], mq_ref[:k_rows, c:c + pair], preferred_element_type=F32)
        if pending is not None:
            finish(*pending)
        pending = (c, y)
    finish(*pending)


def _ssm_out(proj, x_prev, params, d_skip, *, u_col0, batch, seqs):
    in_arrays, out_arrays, in_specs, out_specs, dims = params
    t = proj.shape[0]
    kt, ns, chunk = dims["kt"], dims["ns"], dims["chunk"]
    _, n_slabs, n_rows, _ = x_prev.shape
    cl = chunk * LANES
    assert batch % seqs == 0 and chunk % 2 == 0 and n_slabs * LANES == 2 * ns
    tokens = t // batch * seqs
    tile = lambda k, i: k
    return pl.pallas_call(
        functools.partial(_ssm_out_kernel, chunk=chunk, batch=batch, seqs=seqs,
                          group=dims["group"], state=dims["state"]),
        grid=(kt, batch // seqs),
        in_specs=[
            pl.BlockSpec((tokens, LANES), lambda k, i: (i, u_col0 + k)),
            pl.BlockSpec((1, n_slabs, n_rows, LANES), lambda k, i: (k, 0, 0, 0)),
        ] + in_specs(tile) + out_specs(tile) + [pl.BlockSpec((1, 1, LANES), lambda k, i: (k, 0, 0))],
        out_specs=pl.BlockSpec((tokens, LANES), lambda k, i: (i, k)),
        out_shape=jax.ShapeDtypeStruct((t, kt * LANES), F32),
        scratch_shapes=[pltpu.VMEM((2 * ns + cl, cl), BF16)],
        compiler_params=_params("parallel", "arbitrary"),
        name="ssm_out",
    )(proj, x_prev, *in_arrays, *out_arrays, d_skip.astype(F32).reshape(kt, 1, LANES))


def _outproj_kernel(h_ref, a_ref, y_ref, gw_ref, gb_ref, wa_ref, wy_ref, o_ref, gw_bf, wa_bf, wy_bf):
    @pl.when(pl.program_id(0) == 0)
    def _():
        gw_bf[...] = gw_ref[...].astype(BF16)
        wa_bf[...] = wa_ref[...].astype(BF16)
        wy_bf[...] = wy_ref[...].astype(BF16)

    y = y_ref[...]
    z = jnp.dot(y.astype(BF16), gw_bf[...], preferred_element_type=F32) + gb_ref[...]
    yg = (y * jax.nn.sigmoid(z)).astype(BF16)
    mixed = jnp.dot(a_ref[...], wa_bf[...], preferred_element_type=F32)
    mixed = mixed + jnp.dot(yg, wy_bf[...], preferred_element_type=F32)
    o_ref[...] = h_ref[...] + mixed


def _outproj(h, attn, y, glu_w, glu_b, w_out, *, tm=512):
    t, d = h.shape
    wa = attn.shape[1]
    wy = y.shape[1]
    tm = _tile(t, tm)
    once = dict(pipeline_mode=pl.Buffered(1))
    return pl.pallas_call(
        _outproj_kernel,
        grid=(t // tm,),
        in_specs=[
            pl.BlockSpec((tm, d), lambda i: (i, 0)),
            pl.BlockSpec((tm, wa), lambda i: (i, 0)),
            pl.BlockSpec((tm, wy), lambda i: (i, 0)),
            pl.BlockSpec((wy, wy), lambda i: (0, 0), **once),
            pl.BlockSpec((1, wy), lambda i: (0, 0)),
            pl.BlockSpec((wa, d), lambda i: (0, 0), **once),
            pl.BlockSpec((wy, d), lambda i: (wa // wy, 0), **once),
        ],
        out_specs=pl.BlockSpec((tm, d), lambda i: (i, 0)),
        out_shape=jax.ShapeDtypeStruct((t, d), F32),
        scratch_shapes=[pltpu.VMEM((wy, wy), BF16), pltpu.VMEM((wa, d), BF16), pltpu.VMEM((wy, d), BF16)],
        compiler_params=_params("arbitrary"),
        name="outproj",
    )(h, attn, y, glu_w, glu_b.astype(F32).reshape(1, wy), w_out, w_out)


def kernel(x, ffn1_norm, ffn1_w_gate, ffn1_w_up, ffn1_w_down, mix_norm, w_in, q_norm, k_norm, rel_bias,
           ssm_lambda_re, ssm_lambda_im, ssm_log_dt, ssm_b_re, ssm_b_im, ssm_c_re, ssm_c_im, ssm_d,
           glu_w, glu_b, w_out, ffn2_norm, ffn2_w_gate, ffn2_w_up, ffn2_w_down):
    batch, seq, d_model = x.shape
    depth = w_in.shape[0]
    head_dim = q_norm.shape[-1]
    n_heads = rel_bias.shape[1]
    attn_width = n_heads * head_dim
    ssm_width = glu_w.shape[-1]
    assert w_in.shape[-1] == 3 * attn_width + ssm_width and attn_width % ssm_width == 0
    assert ssm_width % LANES == 0 and seq % SSM_CHUNK == 0
    u_col0 = 3 * attn_width // LANES

    bias = _bias_tables(rel_bias)
    h = x.reshape(batch * seq, d_model).astype(F32)
    for l in range(depth):
        h = _ffn(h, ffn1_norm[l], ffn1_w_gate[l], ffn1_w_up[l], ffn1_w_down[l])
        proj = _proj(h, mix_norm[l], w_in[l].astype(BF16), q_norm[l], k_norm[l], attn_width=attn_width)
        attn = _attention(proj, bias, batch=batch, seq=seq, n_heads=n_heads, head_dim=head_dim)
        ssm = _ssm_params(ssm_lambda_re[l], ssm_lambda_im[l], ssm_log_dt[l],
                          ssm_b_re[l], ssm_b_im[l], ssm_c_re[l], ssm_c_im[l])
        x_prev = _ssm_state(proj, ssm[0], ssm[2], ssm[4], u_col0=u_col0, batch=batch)
        y = _ssm_out(proj, x_prev, ssm, ssm_d[l], u_col0=u_col0, batch=batch, seqs=4 if batch % 4 == 0 else 1)
        h = _outproj(h, attn, y, glu_w[l], glu_b[l], w_out[l])
        h = _ffn(h, ffn2_norm[l], ffn2_w_gate[l], ffn2_w_up[l], ffn2_w_down[l])
    return h.reshape(batch, seq, d_model).astype(x.dtype)
```
